```python
import jax, jax.numpy as jnp
from jax import lax
import numpy as np

D_MODEL = 2048
BATCH = 2
SEQ = 4096
DEPTH = 2

CHUNK = 64
QBLOCK = 128
N_META = 16
PAD = QBLOCK - N_META
N_A = DEPTH // 2
N_B = DEPTH - N_A
GLA_HEADS = 4
GLA_KD = D_MODEL // 2
GLA_VD = D_MODEL
GLA_DK = GLA_KD // GLA_HEADS
GLA_DV = GLA_VD // GLA_HEADS
GATE_RANK = 16
GATE_TAU = 16.0
SB_HEAD_DIM = 128
SB_HEADS = D_MODEL // SB_HEAD_DIM
D_FF = -(-8 * D_MODEL // (3 * 256)) * 256
EPS = 1e-6

kernel_name = 'yoco_gla_stickbreaking_encoder'


def rms_norm(x, g):
    xf = x.astype(jnp.float32)
    y = xf * lax.rsqrt(jnp.mean(xf * xf, axis=-1, keepdims=True) + EPS)
    return (y * g.astype(jnp.float32)).astype(x.dtype)


def swiglu_ffn(h, g, w_gate, w_up, w_down):
    y = rms_norm(h, g)
    return (jax.nn.silu(y @ w_gate) * (y @ w_up)) @ w_down


def gla_mixer(h, valid, g_norm, w_in, w_gate_up, b_gate, g_onorm, w_out):
    B, L, _ = h.shape
    n_c = L // CHUNK
    y = rms_norm(h, g_norm)
    proj = y @ w_in
    q, k, v, g, a = jnp.split(proj, [GLA_KD, 2 * GLA_KD, 2 * GLA_KD + GLA_VD, 2 * GLA_KD + 2 * GLA_VD], axis=-1)
    k = k * valid[None, :, None]
    la = jax.nn.log_sigmoid((a @ w_gate_up + b_gate).astype(jnp.float32)) / GATE_TAU

    def heads(t, dh):
        return t.astype(jnp.float32).reshape(B, n_c, CHUNK, GLA_HEADS, dh).transpose(0, 1, 3, 2, 4)

    q = heads(q, GLA_DK) * GLA_DK ** -0.5
    k = heads(k, GLA_DK)
    v = heads(v, GLA_DV)
    bcum = jnp.cumsum(heads(la, GLA_DK), axis=3)
    ref = bcum[:, :, :, CHUNK // 2 - 1:CHUNK // 2, :]
    s_lo = jnp.einsum('bnhtd,bnhsd->bnhts', q * jnp.exp(bcum - ref), k * jnp.exp(ref - bcum))
    s_up = jnp.einsum('bnhtd,bnhsd->bnhts', q * jnp.exp(ref - bcum), k * jnp.exp(bcum - ref))
    idx = jnp.arange(CHUNK)
    scores = jnp.where(idx[:, None] >= idx[None, :], s_lo, s_up)
    o_intra = jnp.einsum('bnhts,bnhsv->bnhtv', scores, v)
    b_last = bcum[:, :, :, -1:, :]
    k_state = k * jnp.exp(b_last - bcum)
    q_inter = q * jnp.exp(bcum)
    chunk_decay = jnp.exp(b_last[:, :, :, 0, :])

    def step(S, inp):
        qc, kc, vc, dc = inp
        o_c = jnp.einsum('bhtd,bhdv->bhtv', qc, S)
        S_new = dc[..., None] * S + jnp.einsum('bhsd,bhsv->bhdv', kc, vc)
        return S_new, o_c

    xs = (jnp.moveaxis(q_inter, 1, 0), jnp.moveaxis(k_state, 1, 0), jnp.moveaxis(v, 1, 0), jnp.moveaxis(chunk_decay, 1, 0))
    S0 = jnp.zeros((B, GLA_HEADS, GLA_DK, GLA_DV), jnp.float32)
    _, o_inter = lax.scan(step, S0, xs)
    o = o_intra + jnp.moveaxis(o_inter, 0, 1)
    o = rms_norm(o, g_onorm)
    o = o.transpose(0, 1, 3, 2, 4).reshape(B, L, GLA_VD)
    o = o * jax.nn.silu(g.astype(jnp.float32))
    return o.astype(h.dtype) @ w_out


def stick_breaking_mixer(h, k_sh, v_sh, key_valid, g_norm, w_q, g_q, w_o):
    B, L, _ = h.shape
    y = rms_norm(h, g_norm)
    q = rms_norm((y @ w_q).reshape(B, L, SB_HEADS, SB_HEAD_DIM), g_q).astype(jnp.float32) * SB_HEAD_DIM ** -0.5
    pos = jnp.arange(L)
    outs = []
    for blk in range(L // QBLOCK):
        q0 = blk * QBLOCK
        kend = q0 + QBLOCK
        z = jnp.einsum('bqhd,bkhd->bhqk', q[:, q0:kend], k_sh[:, :kend])
        valid = (pos[None, :kend] < pos[q0:kend, None]) & key_valid[None, :kend]
        l_fail = jnp.where(valid, jax.nn.log_sigmoid(-z), 0.0)
        suffix = lax.cumsum(l_fail, axis=3, reverse=True) - l_fail
        att = jnp.where(valid, jnp.exp(jax.nn.log_sigmoid(z) + suffix), 0.0)
        outs.append(jnp.einsum('bhqk,bkhd->bqhd', att, v_sh[:, :kend]))
    o = jnp.concatenate(outs, axis=1).reshape(B, L, D_MODEL)
    return o.astype(h.dtype) @ w_o


def setup_inputs(seed: int = 0) -> dict:
    key = jax.random.key(seed)
    ks = jax.random.split(key, 19)
    f32 = jnp.float32

    def dense(k, shape, fan_in):
        return jax.random.normal(k, shape, f32) * fan_in ** -0.5

    def gain(k, shape):
        return 1.0 + 0.02 * jax.random.normal(k, shape, f32)

    n_in_a = 2 * GLA_KD + 2 * GLA_VD + GATE_RANK
    return {
        'x': jax.random.normal(ks[0], (BATCH, SEQ, D_MODEL), f32),
        'meta_tokens': jax.random.normal(ks[1], (N_META, D_MODEL), f32),
        'g_norm_a': gain(ks[2], (N_A, D_MODEL)),
        'w_in_a': dense(ks[3], (N_A, D_MODEL, n_in_a), D_MODEL),
        'w_gate_up_a': dense(ks[4], (N_A, GATE_RANK, GLA_KD), GATE_RANK),
        'b_gate_a': 0.1 * jax.random.normal(ks[5], (N_A, GLA_KD), f32),
        'g_onorm_a': gain(ks[6], (N_A, GLA_DV)),
        'w_out_a': dense(ks[7], (N_A, GLA_VD, D_MODEL), GLA_VD),
        'g_kv_norm': gain(ks[8], (D_MODEL,)),
        'w_kv': dense(ks[9], (D_MODEL, 2 * D_MODEL), D_MODEL),
        'g_k': gain(ks[10], (SB_HEAD_DIM,)),
        'g_norm_b': gain(ks[11], (N_B, D_MODEL)),
        'w_q_b': dense(ks[12], (N_B, D_MODEL, D_MODEL), D_MODEL),
        'g_q_b': gain(ks[13], (N_B, SB_HEAD_DIM)),
        'w_o_b': dense(ks[14], (N_B, D_MODEL, D_MODEL), D_MODEL),
        'g_ffn_norm': gain(ks[15], (DEPTH, D_MODEL)),
        'w_ffn_gate': dense(ks[16], (DEPTH, D_MODEL, D_FF), D_MODEL),
        'w_ffn_up': dense(ks[17], (DEPTH, D_MODEL, D_FF), D_MODEL),
        'w_ffn_down': dense(ks[18], (DEPTH, D_FF, D_MODEL), D_FF),
    }


def reference(x, meta_tokens, g_norm_a, w_in_a, w_gate_up_a, b_gate_a, g_onorm_a, w_out_a,
              g_kv_norm, w_kv, g_k, g_norm_b, w_q_b, g_q_b, w_o_b,
              g_ffn_norm, w_ffn_gate, w_ffn_up, w_ffn_down):
    B = x.shape[0]
    h = jnp.concatenate([
        jnp.zeros((B, PAD, D_MODEL), x.dtype),
        jnp.broadcast_to(meta_tokens.astype(x.dtype)[None], (B, N_META, D_MODEL)),
        x,
    ], axis=1)
    L = h.shape[1]
    key_valid = jnp.arange(L) >= PAD
    valid = key_valid.astype(x.dtype)
    k_sh = v_sh = None
    for layer in range(DEPTH):
        if layer < N_A:
            h = h + gla_mixer(h, valid, g_norm_a[layer], w_in_a[layer], w_gate_up_a[layer],
                              b_gate_a[layer], g_onorm_a[layer], w_out_a[layer])
        else:
            if layer == N_A:
                kv = rms_norm(h, g_kv_norm) @ w_kv
                k_sh, v_sh = jnp.split(kv, 2, axis=-1)
                k_sh = rms_norm(k_sh.reshape(B, L, SB_HEADS, SB_HEAD_DIM), g_k).astype(jnp.float32)
                v_sh = v_sh.reshape(B, L, SB_HEADS, SB_HEAD_DIM).astype(jnp.float32)
            j = layer - N_A
            h = h + stick_breaking_mixer(h, k_sh, v_sh, key_valid, g_norm_b[j], w_q_b[j], g_q_b[j], w_o_b[j])
        h = h + swiglu_ffn(h, g_ffn_norm[layer], w_ffn_gate[layer], w_ffn_up[layer], w_ffn_down[layer])
    return h[:, PAD + N_META:]
```

```python
import functools

import jax
import jax.numpy as jnp
from jax import lax
from jax.experimental import pallas as pl
from jax.experimental.pallas import tpu as pltpu

F32 = jnp.float32
BF16 = jnp.bfloat16

CHUNK = 64
N_META = 16
LEAD = 128
PAD = LEAD - N_META
GLA_HEADS = 4
GATE_RANK = 16
GATE_TAU = 16.0
SB_HEAD_DIM = 128
EPS = 1e-6

LANES = 128
VMEM_LIMIT = 56 * 1024 * 1024

SB_DEAD_LOG = -104.0
SB_TQ = 256
SB_HG = 4


def _params(sem):
    return pltpu.CompilerParams(dimension_semantics=sem, vmem_limit_bytes=VMEM_LIMIT)


def _dot(a, b):
    return jnp.dot(a, b, preferred_element_type=F32)


def _dot_nt(a, b):
    return lax.dot_general(a, b, (((1,), (1,)), ((), ())), preferred_element_type=F32)


def _dot_tn(a, b):
    return lax.dot_general(a, b, (((0,), (0,)), ((), ())), preferred_element_type=F32)


def _rms(x, g):
    ms = jnp.mean(x * x, axis=-1, keepdims=True)
    return x * lax.rsqrt(ms + EPS) * g


def _log_sigmoid(x):
    return jnp.minimum(x, 0.0) - jnp.log(1.0 + jnp.exp(-jnp.abs(x)))


def _split_bf16(x):
    hi = x.astype(BF16)
    lo = (x - hi.astype(F32)).astype(BF16)
    return hi, lo


def _head_rms(x, g, scale):
    outs = []
    for h in range(x.shape[1] // LANES):
        xs = x[:, h * LANES:(h + 1) * LANES]
        ms = jnp.mean(xs * xs, axis=-1, keepdims=True)
        outs.append(xs * lax.rsqrt(ms + EPS) * g * scale)
    return outs


def _gla_proj_kernel(h_ref, g_ref, w_ref, wa_ref, wgu_ref, bg_ref, proj_ref, la_ref, y_scr):
    @pl.when(pl.program_id(1) == 0)
    def _():
        y = _rms(h_ref[...], g_ref[...]).astype(BF16)
        y_scr[...] = y
        a = _dot(y, wa_ref[...])
        z = _dot(a.astype(BF16), wgu_ref[...]) + bg_ref[...]
        la_ref[...] = _log_sigmoid(z) * (1.0 / GATE_TAU)

    proj_ref[...] = _dot(y_scr[...], w_ref[...])


def _gla_proj(h, g, w_main, w_a, w_gu, b_gate, tm, tn):
    m, d = h.shape
    n = w_main.shape[1]
    kd = w_gu.shape[1]
    return pl.pallas_call(
        _gla_proj_kernel,
        grid=(m // tm, n // tn),
        in_specs=[
            pl.BlockSpec((tm, d), lambda i, j: (i, 0)),
            pl.BlockSpec((1, d), lambda i, j: (0, 0)),
            pl.BlockSpec((d, tn), lambda i, j: (0, j)),
            pl.BlockSpec((d, LANES), lambda i, j: (0, 0)),
            pl.BlockSpec((LANES, kd), lambda i, j: (0, 0)),
            pl.BlockSpec((1, kd), lambda i, j: (0, 0)),
        ],
        out_specs=[
            pl.BlockSpec((tm, tn), lambda i, j: (i, j)),
            pl.BlockSpec((tm, kd), lambda i, j: (i, 0)),
        ],
        out_shape=[
            jax.ShapeDtypeStruct((m, n), F32),
            jax.ShapeDtypeStruct((m, kd), F32),
        ],
        scratch_shapes=[pltpu.VMEM((tm, d), BF16)],
        compiler_params=_params(("parallel", "arbitrary")),
        name="gla_proj",
    )(h, g, w_main, w_a, w_gu, b_gate)


def _gla_core_kernel(q_ref, k_ref, v_ref, g_ref, la_ref, gon_ref, o_ref, st_scr, *, dk, dv):
    c = pl.program_id(1)

    @pl.when(c == 0)
    def _():
        st_scr[...] = jnp.zeros_like(st_scr)

    row = lax.broadcasted_iota(jnp.int32, (CHUNK, CHUNK), 0)
    col = lax.broadcasted_iota(jnp.int32, (CHUNK, CHUNK), 1)
    tri = (col <= row).astype(BF16)
    lower = row >= col
    pos = c * CHUNK + lax.broadcasted_iota(jnp.int32, (CHUNK, 1), 0)
    key_ok = pos >= PAD

    la_hi, la_lo = _split_bf16(la_ref[0])
    bcum_all = _dot(tri, la_hi) + _dot(tri, la_lo)
    gon = gon_ref[...]

    for hd in range(GLA_HEADS):
        ks = slice(hd * dk, (hd + 1) * dk)
        vs = slice(hd * dv, (hd + 1) * dv)
        q = q_ref[0, :, ks] * (dk ** -0.5)
        k = jnp.where(key_ok, k_ref[0, :, ks], 0.0)
        v = v_ref[0, :, vs].astype(BF16)
        b = bcum_all[:, ks]
        b_ref = b[CHUNK // 2 - 1:CHUNK // 2, :]
        b_last = b[CHUNK - 1:CHUNK, :]
        e_pos = jnp.exp(b - b_ref)
        e_neg = jnp.exp(b_ref - b)
        s_lo = _dot_nt((q * e_pos).astype(BF16), (k * e_neg).astype(BF16))
        s_up = _dot_nt((q * e_neg).astype(BF16), (k * e_pos).astype(BF16))
        scores = jnp.where(lower, s_lo, s_up).astype(BF16)
        st = st_scr[hd]
        o = _dot(scores, v) + _dot_nt((q * jnp.exp(b)).astype(BF16), st.astype(BF16))
        k_state = (k * jnp.exp(b_last - b)).astype(BF16)
        st_scr[hd] = st * jnp.exp(b_last) + _dot_tn(v, k_state)
        o = _rms(o, gon)
        gate = g_ref[0, :, vs]
        o_ref[0, :, vs] = (o * (gate * jax.nn.sigmoid(gate))).astype(o_ref.dtype)


def _gla_core(proj, la, g_onorm, batch, length):
    kd = la.shape[-1]
    vd = (proj.shape[-1] - 2 * kd) // 2
    dk, dv = kd // GLA_HEADS, vd // GLA_HEADS
    n_c = length // CHUNK
    shift = n_c - LEAD // CHUNK
    proj = proj.reshape(batch, length, proj.shape[-1])
    la = la.reshape(batch, length, kd)

    def blk(col):
        return lambda b, c: (b, (c + shift) % n_c, col)

    return pl.pallas_call(
        functools.partial(_gla_core_kernel, dk=dk, dv=dv),
        grid=(batch, n_c),
        in_specs=[
            pl.BlockSpec((1, CHUNK, kd), blk(0)),
            pl.BlockSpec((1, CHUNK, kd), blk(1)),
            pl.BlockSpec((1, CHUNK, vd), blk(kd * 2 // vd)),
            pl.BlockSpec((1, CHUNK, vd), blk(kd * 2 // vd + 1)),
            pl.BlockSpec((1, CHUNK, kd), blk(0)),
            pl.BlockSpec((1, dv), lambda b, c: (0, 0)),
        ],
        out_specs=pl.BlockSpec((1, CHUNK, vd), blk(0)),
        out_shape=jax.ShapeDtypeStruct((batch, length, vd), BF16),
        scratch_shapes=[pltpu.VMEM((GLA_HEADS, dv, dk), F32)],
        compiler_params=_params(("parallel", "arbitrary")),
        name="gla_core",
    )(proj, proj, proj, proj, la, g_onorm)


def _proj_res_kernel(x_ref, w_ref, res_ref, o_ref):
    o_ref[0] = res_ref[0] + _dot(x_ref[0], w_ref[...])


def _proj_res(x, w, res, tm, tn):
    batch, rows, d = x.shape
    n = w.shape[1]
    return pl.pallas_call(
        _proj_res_kernel,
        grid=(batch, rows // tm, n // tn),
        in_specs=[
            pl.BlockSpec((1, tm, d), lambda b, i, j: (b, i, 0)),
            pl.BlockSpec((d, tn), lambda b, i, j: (0, j)),
            pl.BlockSpec((1, tm, tn), lambda b, i, j: (b, i, j)),
        ],
        out_specs=pl.BlockSpec((1, tm, tn), lambda b, i, j: (b, i, j)),
        out_shape=jax.ShapeDtypeStruct((batch, rows, n), F32),
        compiler_params=_params(("parallel", "parallel", "arbitrary")),
        name="proj_res",
    )(x, w, res)


def _ffn_kernel(h_ref, g_ref, wg_ref, wu_ref, wd_ref, o_ref, y_scr):
    @pl.when(pl.program_id(2) == 0)
    def _():
        h = h_ref[0]
        y_scr[...] = _rms(h, g_ref[...]).astype(BF16)
        o_ref[0] = h

    y = y_scr[...]
    gate = _dot(y, wg_ref[...])
    up = _dot(y, wu_ref[...])
    act = (gate * jax.nn.sigmoid(gate) * up).astype(BF16)
    o_ref[0] += _dot(act, wd_ref[...])


def _ffn(h, g, w_gate, w_up, w_down, rows, tm, tf):
    batch, _, d = h.shape
    d_ff = w_gate.shape[1]
    return pl.pallas_call(
        _ffn_kernel,
        grid=(batch, rows // tm, d_ff // tf),
        in_specs=[
            pl.BlockSpec((1, tm, d), lambda b, i, f: (b, i, 0)),
            pl.BlockSpec((1, d), lambda b, i, f: (0, 0)),
            pl.BlockSpec((d, tf), lambda b, i, f: (0, f)),
            pl.BlockSpec((d, tf), lambda b, i, f: (0, f)),
            pl.BlockSpec((tf, d), lambda b, i, f: (f, 0)),
        ],
        out_specs=pl.BlockSpec((1, tm, d), lambda b, i, f: (b, i, 0)),
        out_shape=jax.ShapeDtypeStruct((batch, rows, d), F32),
        scratch_shapes=[pltpu.VMEM((tm, d), BF16)],
        compiler_params=_params(("parallel", "parallel", "arbitrary")),
        name="ffn",
    )(h, g, w_gate, w_up, w_down)


def _kv_proj_kernel(h_ref, g_ref, w_ref, gk_ref, o_ref, y_scr, *, n_key_tiles):
    j = pl.program_id(2)

    @pl.when(j == 0)
    def _():
        y_scr[...] = _rms(h_ref[0], g_ref[...]).astype(BF16)

    r = _dot(y_scr[...], w_ref[...])

    @pl.when(j < n_key_tiles)
    def _():
        o_ref[0] = jnp.concatenate(_head_rms(r, gk_ref[...], 1.0), axis=1).astype(o_ref.dtype)

    @pl.when(j >= n_key_tiles)
    def _():
        o_ref[0] = r.astype(o_ref.dtype)


def _kv_proj(h, g, w_kv, g_k, tm, tn):
    batch, rows, d = h.shape
    n = w_kv.shape[1]
    return pl.pallas_call(
        functools.partial(_kv_proj_kernel, n_key_tiles=(n // 2) // tn),
        grid=(batch, rows // tm, n // tn),
        in_specs=[
            pl.BlockSpec((1, tm, d), lambda b, i, j: (b, i, 0)),
            pl.BlockSpec((1, d), lambda b, i, j: (0, 0)),
            pl.BlockSpec((d, tn), lambda b, i, j: (0, j)),
            pl.BlockSpec((1, SB_HEAD_DIM), lambda b, i, j: (0, 0)),
        ],
        out_specs=pl.BlockSpec((1, tm, tn), lambda b, i, j: (b, i, j)),
        out_shape=jax.ShapeDtypeStruct((batch, rows, n), BF16),
        scratch_shapes=[pltpu.VMEM((tm, d), BF16)],
        compiler_params=_params(("parallel", "parallel", "arbitrary")),
        name="kv_proj",
    )(h, g, w_kv, g_k)


def _q_proj_kernel(h_ref, g_ref, w_ref, gq_ref, o_ref, y_scr):
    @pl.when(pl.program_id(2) == 0)
    def _():
        y_scr[...] = _rms(h_ref[0], g_ref[...]).astype(BF16)

    r = _dot(y_scr[...], w_ref[...])
    heads = _head_rms(r, gq_ref[...], SB_HEAD_DIM ** -0.5)
    o_ref[0] = jnp.concatenate(heads, axis=1).astype(o_ref.dtype)


def _q_proj(h, g, w_q, g_q, rows, tm, tn):
    batch, _, d = h.shape
    n = w_q.shape[1]
    return pl.pallas_call(
        _q_proj_kernel,
        grid=(batch, rows // tm, n // tn),
        in_specs=[
            pl.BlockSpec((1, tm, d), lambda b, i, j: (b, i, 0)),
            pl.BlockSpec((1, d), lambda b, i, j: (0, 0)),
            pl.BlockSpec((d, tn), lambda b, i, j: (0, j)),
            pl.BlockSpec((1, SB_HEAD_DIM), lambda b, i, j: (0, 0)),
        ],
        out_specs=pl.BlockSpec((1, tm, tn), lambda b, i, j: (b, i, j)),
        out_shape=jax.ShapeDtypeStruct((batch, rows, n), BF16),
        scratch_shapes=[pltpu.VMEM((tm, d), BF16)],
        compiler_params=_params(("parallel", "parallel", "arbitrary")),
        name="q_proj",
    )(h, g, w_q, g_q)


def _sb_kernel(q_ref, k_ref, v_ref, u_ref, o_ref, acc_scr, car_scr, *, n_kblk):
    i = pl.program_id(2)
    tq = q_ref.shape[1]
    d = SB_HEAD_DIM
    acc_scr[...] = jnp.zeros_like(acc_scr)
    car_scr[...] = jnp.zeros_like(car_scr)
    q_pos = LEAD + i * tq + lax.broadcasted_iota(jnp.int32, (tq, d), 0)
    k_off = lax.broadcasted_iota(jnp.int32, (tq, d), 1)
    top = (LEAD + tq) // d - 1 + i * (tq // d)

    def cond(state):
        jb, alive = state
        return jnp.logical_and(jb >= 0, alive > SB_DEAD_LOG)

    def body(state):
        jb, _ = state
        start = pl.multiple_of(((jb + n_kblk - 1) % n_kblk) * d, d)
        k_pos = jb * d + k_off
        valid = jnp.logical_and(k_pos < q_pos, k_pos >= PAD)
        u = u_ref[...]
        alive = jnp.full((), -jnp.inf, F32)
        for hd in range(SB_HG):
            hs = slice(hd * d, (hd + 1) * d)
            kb = k_ref[0, pl.ds(start, d), hs]
            vb = v_ref[0, pl.ds(start, d), hs]
            z = _dot_nt(q_ref[0, :, hs], kb)
            lf = jnp.where(valid, _log_sigmoid(-z), 0.0)
            hi, lo = _split_bf16(lf)
            cs = _dot(hi, u) + _dot(lo, u)
            carry = car_scr[hd]
            att = jnp.where(valid, jnp.exp(lf + z + cs[:, :d] + carry), 0.0)
            acc_scr[hd] += _dot(att.astype(BF16), vb)
            carry = carry + cs[:, d:]
            car_scr[hd] = carry
            alive = jnp.maximum(alive, jnp.max(carry))
        return jb - 1, alive

    lax.while_loop(cond, body, (top, jnp.zeros((), F32)))
    for hd in range(SB_HG):
        o_ref[0, :, hd * d:(hd + 1) * d] = acc_scr[hd].astype(o_ref.dtype)


def _sb_attn(q, kv, u):
    batch, rows, dm = q.shape
    length = kv.shape[1]
    gw = SB_HG * SB_HEAD_DIM
    return pl.pallas_call(
        functools.partial(_sb_kernel, n_kblk=length // SB_HEAD_DIM),
        grid=(batch, dm // gw, rows // SB_TQ),
        in_specs=[
            pl.BlockSpec((1, SB_TQ, gw), lambda b, h, i: (b, i, h)),
            pl.BlockSpec((1, length, gw), lambda b, h, i: (b, 0, h)),
            pl.BlockSpec((1, length, gw), lambda b, h, i: (b, 0, h + dm // gw)),
            pl.BlockSpec((SB_HEAD_DIM, 2 * SB_HEAD_DIM), lambda b, h, i: (0, 0)),
        ],
        out_specs=pl.BlockSpec((1, SB_TQ, gw), lambda b, h, i: (b, i, h)),
        out_shape=jax.ShapeDtypeStruct((batch, rows, dm), BF16),
        scratch_shapes=[
            pltpu.VMEM((SB_HG, SB_TQ, SB_HEAD_DIM), F32),
            pltpu.VMEM((SB_HG, SB_TQ, SB_HEAD_DIM), F32),
        ],
        compiler_params=_params(("parallel", "parallel", "arbitrary")),
        name="sb_attn",
    )(q, kv, kv, u)


def _suffix_matrix():
    r = jnp.arange(SB_HEAD_DIM)
    later = (r[:, None] > r[None, :]).astype(BF16)
    return jnp.concatenate([later, jnp.ones((SB_HEAD_DIM, SB_HEAD_DIM), BF16)], axis=1)


def kernel(x, meta_tokens, g_norm_a, w_in_a, w_gate_up_a, b_gate_a, g_onorm_a, w_out_a, g_kv_norm, w_kv, g_k, g_norm_b, w_q_b, g_q_b, w_o_b, g_ffn_norm, w_ffn_gate, w_ffn_up, w_ffn_down):
    batch, seq, d = x.shape
    length = seq + LEAD
    n_a = g_norm_a.shape[0]
    n_b = g_norm_b.shape[0]
    kd = w_gate_up_a.shape[-1]
    n_main = w_in_a.shape[-1] - GATE_RANK

    lead = jnp.concatenate([jnp.zeros((PAD, d), x.dtype), meta_tokens.astype(x.dtype)], axis=0)
    h = jnp.concatenate([x, jnp.broadcast_to(lead[None], (batch, LEAD, d))], axis=1)

    def row(v):
        return v.reshape(1, -1).astype(F32)

    for layer in range(n_a):
        w_in = w_in_a[layer]
        w_a = jnp.pad(w_in[:, n_main:], ((0, 0), (0, LANES - GATE_RANK))).astype(BF16)
        w_gu = jnp.pad(w_gate_up_a[layer], ((0, LANES - GATE_RANK), (0, 0))).astype(BF16)
        proj, la = _gla_proj(h.reshape(batch * length, d), row(g_norm_a[layer]),
                             w_in[:, :n_main].astype(BF16), w_a, w_gu, row(b_gate_a[layer]),
                             tm=1056, tn=1024)
        o = _gla_core(proj, la, row(g_onorm_a[layer]), batch, length)
        h = _proj_res(o, w_out_a[layer].astype(BF16), h, tm=1056, tn=1024)
        h = _ffn(h, row(g_ffn_norm[layer]), w_ffn_gate[layer].astype(BF16),
                 w_ffn_up[layer].astype(BF16), w_ffn_down[layer].astype(BF16),
                 rows=length, tm=704, tf=512)

    kv = _kv_proj(h, row(g_kv_norm), w_kv.astype(BF16), row(g_k), tm=1056, tn=1024)
    u = _suffix_matrix()
    for j in range(n_b):
        layer = n_a + j
        q = _q_proj(h, row(g_norm_b[j]), w_q_b[j].astype(BF16), row(g_q_b[j]),
                    rows=seq, tm=1024, tn=1024)
        o = _sb_attn(q, kv, u)
        h = _proj_res(o, w_o_b[j].astype(BF16), h, tm=1024, tn=1024)
        h = _ffn(h, row(g_ffn_norm[layer]), w_ffn_gate[layer].astype(BF16),
                 w_ffn_up[layer].astype(BF16), w_ffn_down[layer].astype(BF16),
                 rows=seq, tm=512, tf=512)
    return h
```

```python
import functools

import jax
import jax.numpy as jnp
from jax import lax
from jax.experimental import pallas as pl
from jax.experimental.pallas import tpu as pltpu

F32 = jnp.float32
BF16 = jnp.bfloat16

CHUNK = 64
N_META = 16
LEAD = 128
PAD = LEAD - N_META
GLA_HEADS = 4
GATE_RANK = 16
GATE_TAU = 16.0
SB_HEAD_DIM = 128
EPS = 1e-6

LANES = 128
VMEM_LIMIT = 56 * 1024 * 1024

SB_DEAD_LOG = -104.0
SB_TQ = SB_HEAD_DIM
SB_HG = 8


def _params(sem):
    return pltpu.CompilerParams(dimension_semantics=sem, vmem_limit_bytes=VMEM_LIMIT)


def _dot(a, b):
    return jnp.dot(a, b, preferred_element_type=F32)


def _dot_nt(a, b):
    return lax.dot_general(a, b, (((1,), (1,)), ((), ())), preferred_element_type=F32)


def _dot_tn(a, b):
    return lax.dot_general(a, b, (((0,), (0,)), ((), ())), preferred_element_type=F32)


def _rms(x, g):
    ms = jnp.mean(x * x, axis=-1, keepdims=True)
    return x * lax.rsqrt(ms + EPS) * g


def _log_sigmoid(x):
    return jnp.minimum(x, 0.0) - jnp.log(1.0 + jnp.exp(-jnp.abs(x)))


def _split_bf16(x):
    hi = x.astype(BF16)
    lo = (x - hi.astype(F32)).astype(BF16)
    return hi, lo


def _head_rms(x, g, scale):
    outs = []
    for h in range(x.shape[1] // LANES):
        xs = x[:, h * LANES:(h + 1) * LANES]
        ms = jnp.mean(xs * xs, axis=-1, keepdims=True)
        outs.append(xs * lax.rsqrt(ms + EPS) * g * scale)
    return outs


def _gla_proj_kernel(h_ref, g_ref, w_ref, wa_ref, wgu_ref, bg_ref, proj_ref, la_ref, y_scr):
    @pl.when(pl.program_id(1) == 0)
    def _():
        y = _rms(h_ref[...], g_ref[...]).astype(BF16)
        y_scr[...] = y
        a = _dot(y, wa_ref[...])
        z = _dot(a.astype(BF16), wgu_ref[...]) + bg_ref[...]
        la_ref[...] = _log_sigmoid(z) * (1.0 / GATE_TAU)

    proj_ref[...] = _dot(y_scr[...], w_ref[...])


def _gla_proj(h, g, w_in, layer, w_a, w_gu, b_gate, tm, tn):
    m, d = h.shape
    n = w_in.shape[-1] - GATE_RANK
    kd = w_gu.shape[1]
    return pl.pallas_call(
        _gla_proj_kernel,
        grid=(m // tm, n // tn),
        in_specs=[
            pl.BlockSpec((tm, d), lambda i, j: (i, 0)),
            pl.BlockSpec((1, d), lambda i, j: (0, 0)),
            pl.BlockSpec((None, d, tn), lambda i, j: (layer, 0, j)),
            pl.BlockSpec((d, LANES), lambda i, j: (0, 0)),
            pl.BlockSpec((LANES, kd), lambda i, j: (0, 0)),
            pl.BlockSpec((1, kd), lambda i, j: (0, 0)),
        ],
        out_specs=[
            pl.BlockSpec((tm, tn), lambda i, j: (i, j)),
            pl.BlockSpec((tm, kd), lambda i, j: (i, 0)),
        ],
        out_shape=[
            jax.ShapeDtypeStruct((m, n), F32),
            jax.ShapeDtypeStruct((m, kd), F32),
        ],
        scratch_shapes=[pltpu.VMEM((tm, d), BF16)],
        compiler_params=_params(("parallel", "arbitrary")),
        name="gla_proj",
    )(h, g, w_in, w_a, w_gu, b_gate)


def _gla_core_kernel(q_ref, k_ref, v_ref, g_ref, la_ref, gon_ref, o_ref, st_scr, *, dk, dv):
    c = pl.program_id(1)

    @pl.when(c == 0)
    def _():
        st_scr[...] = jnp.zeros_like(st_scr)

    row = lax.broadcasted_iota(jnp.int32, (CHUNK, CHUNK), 0)
    col = lax.broadcasted_iota(jnp.int32, (CHUNK, CHUNK), 1)
    tri = (col <= row).astype(BF16)
    lower = row >= col
    pos = c * CHUNK + lax.broadcasted_iota(jnp.int32, (CHUNK, 1), 0)
    key_ok = pos >= PAD

    la_hi, la_lo = _split_bf16(la_ref[0])
    bcum_all = _dot(tri, la_hi) + _dot(tri, la_lo)
    gon = gon_ref[...]

    for hd in range(GLA_HEADS):
        ks = slice(hd * dk, (hd + 1) * dk)
        vs = slice(hd * dv, (hd + 1) * dv)
        q = q_ref[0, :, ks] * (dk ** -0.5)
        k = jnp.where(key_ok, k_ref[0, :, ks], 0.0)
        v = v_ref[0, :, vs].astype(BF16)
        b = bcum_all[:, ks]
        b_ref = b[CHUNK // 2 - 1:CHUNK // 2, :]
        b_last = b[CHUNK - 1:CHUNK, :]
        e_pos = jnp.exp(b - b_ref)
        e_neg = jnp.exp(b_ref - b)
        s_lo = _dot_nt((q * e_pos).astype(BF16), (k * e_neg).astype(BF16))
        s_up = _dot_nt((q * e_neg).astype(BF16), (k * e_pos).astype(BF16))
        scores = jnp.where(lower, s_lo, s_up).astype(BF16)
        st = st_scr[hd]
        o = _dot(scores, v) + _dot_nt((q * jnp.exp(b)).astype(BF16), st.astype(BF16))
        k_state = (k * jnp.exp(b_last - b)).astype(BF16)
        st_scr[hd] = st * jnp.exp(b_last) + _dot_tn(v, k_state)
        o = _rms(o, gon)
        gate = g_ref[0, :, vs]
        o_ref[0, :, vs] = (o * (gate * jax.nn.sigmoid(gate))).astype(o_ref.dtype)


def _gla_core(proj, la, g_onorm, batch, length):
    kd = la.shape[-1]
    vd = (proj.shape[-1] - 2 * kd) // 2
    dk, dv = kd // GLA_HEADS, vd // GLA_HEADS
    n_c = length // CHUNK
    shift = n_c - LEAD // CHUNK
    proj = proj.reshape(batch, length, proj.shape[-1])
    la = la.reshape(batch, length, kd)

    def blk(col):
        return lambda b, c: (b, (c + shift) % n_c, col)

    return pl.pallas_call(
        functools.partial(_gla_core_kernel, dk=dk, dv=dv),
        grid=(batch, n_c),
        in_specs=[
            pl.BlockSpec((1, CHUNK, kd), blk(0)),
            pl.BlockSpec((1, CHUNK, kd), blk(1)),
            pl.BlockSpec((1, CHUNK, vd), blk(kd * 2 // vd)),
            pl.BlockSpec((1, CHUNK, vd), blk(kd * 2 // vd + 1)),
            pl.BlockSpec((1, CHUNK, kd), blk(0)),
            pl.BlockSpec((1, dv), lambda b, c: (0, 0)),
        ],
        out_specs=pl.BlockSpec((1, CHUNK, vd), blk(0)),
        out_shape=jax.ShapeDtypeStruct((batch, length, vd), BF16),
        scratch_shapes=[pltpu.VMEM((GLA_HEADS, dv, dk), F32)],
        compiler_params=_params(("parallel", "arbitrary")),
        name="gla_core",
    )(proj, proj, proj, proj, la, g_onorm)


def _proj_res_kernel(x_ref, w_ref, res_ref, o_ref):
    o_ref[0] = res_ref[0] + _dot(x_ref[0], w_ref[...])


def _proj_res(x, w, res, tm, tn):
    batch, rows, d = x.shape
    n = w.shape[1]
    return pl.pallas_call(
        _proj_res_kernel,
        grid=(batch, rows // tm, n // tn),
        in_specs=[
            pl.BlockSpec((1, tm, d), lambda b, i, j: (b, i, 0)),
            pl.BlockSpec((d, tn), lambda b, i, j: (0, j)),
            pl.BlockSpec((1, tm, tn), lambda b, i, j: (b, i, j)),
        ],
        out_specs=pl.BlockSpec((1, tm, tn), lambda b, i, j: (b, i, j)),
        out_shape=jax.ShapeDtypeStruct((batch, rows, n), F32),
        compiler_params=_params(("parallel", "parallel", "arbitrary")),
        name="proj_res",
    )(x, w, res)


def _ffn_kernel(h_ref, g_ref, wg_ref, wu_ref, wd_ref, o_ref, y_scr):
    @pl.when(pl.program_id(2) == 0)
    def _():
        h = h_ref[0]
        y_scr[...] = _rms(h, g_ref[...]).astype(BF16)
        o_ref[0] = h

    y = y_scr[...]
    gate = _dot(y, wg_ref[...])
    up = _dot(y, wu_ref[...])
    act = (gate * jax.nn.sigmoid(gate) * up).astype(BF16)
    o_ref[0] += _dot(act, wd_ref[...])


def _ffn(h, g, w_gate, w_up, w_down, layer, rows, tm, tf):
    batch, _, d = h.shape
    d_ff = w_gate.shape[-1]
    return pl.pallas_call(
        _ffn_kernel,
        grid=(batch, rows // tm, d_ff // tf),
        in_specs=[
            pl.BlockSpec((1, tm, d), lambda b, i, f: (b, i, 0)),
            pl.BlockSpec((1, d), lambda b, i, f: (0, 0)),
            pl.BlockSpec((None, d, tf), lambda b, i, f: (layer, 0, f)),
            pl.BlockSpec((None, d, tf), lambda b, i, f: (layer, 0, f)),
            pl.BlockSpec((None, tf, d), lambda b, i, f: (layer, f, 0)),
        ],
        out_specs=pl.BlockSpec((1, tm, d), lambda b, i, f: (b, i, 0)),
        out_shape=jax.ShapeDtypeStruct((batch, rows, d), F32),
        scratch_shapes=[pltpu.VMEM((tm, d), BF16)],
        compiler_params=_params(("parallel", "parallel", "arbitrary")),
        name="ffn",
    )(h, g, w_gate, w_up, w_down)


def _kv_proj_kernel(h_ref, g_ref, w_ref, gk_ref, o_ref, y_scr, *, n_key_tiles):
    j = pl.program_id(2)

    @pl.when(j == 0)
    def _():
        y_scr[...] = _rms(h_ref[0], g_ref[...]).astype(BF16)

    r = _dot(y_scr[...], w_ref[...])
    normed = jnp.concatenate(_head_rms(r, gk_ref[...], 1.0), axis=1)
    o_ref[0] = jnp.where(j < n_key_tiles, normed, r).astype(o_ref.dtype)


def _kv_proj(h, g, w_kv, g_k, tm, tn):
    batch, rows, d = h.shape
    n = w_kv.shape[1]
    return pl.pallas_call(
        functools.partial(_kv_proj_kernel, n_key_tiles=(n // 2) // tn),
        grid=(batch, rows // tm, n // tn),
        in_specs=[
            pl.BlockSpec((1, tm, d), lambda b, i, j: (b, i, 0)),
            pl.BlockSpec((1, d), lambda b, i, j: (0, 0)),
            pl.BlockSpec((d, tn), lambda b, i, j: (0, j)),
            pl.BlockSpec((1, SB_HEAD_DIM), lambda b, i, j: (0, 0)),
        ],
        out_specs=pl.BlockSpec((1, tm, tn), lambda b, i, j: (b, i, j)),
        out_shape=jax.ShapeDtypeStruct((batch, rows, n), BF16),
        scratch_shapes=[pltpu.VMEM((tm, d), BF16)],
        compiler_params=_params(("parallel", "parallel", "arbitrary")),
        name="kv_proj",
    )(h, g, w_kv, g_k)


def _q_proj_kernel(h_ref, g_ref, w_ref, gq_ref, o_ref, y_scr):
    @pl.when(pl.program_id(2) == 0)
    def _():
        y_scr[...] = _rms(h_ref[0], g_ref[...]).astype(BF16)

    r = _dot(y_scr[...], w_ref[...])
    heads = _head_rms(r, gq_ref[...], SB_HEAD_DIM ** -0.5)
    o_ref[0] = jnp.concatenate(heads, axis=1).astype(o_ref.dtype)


def _q_proj(h, g, w_q, g_q, rows, tm, tn):
    batch, _, d = h.shape
    n = w_q.shape[1]
    return pl.pallas_call(
        _q_proj_kernel,
        grid=(batch, rows // tm, n // tn),
        in_specs=[
            pl.BlockSpec((1, tm, d), lambda b, i, j: (b, i, 0)),
            pl.BlockSpec((1, d), lambda b, i, j: (0, 0)),
            pl.BlockSpec((d, tn), lambda b, i, j: (0, j)),
            pl.BlockSpec((1, SB_HEAD_DIM), lambda b, i, j: (0, 0)),
        ],
        out_specs=pl.BlockSpec((1, tm, tn), lambda b, i, j: (b, i, j)),
        out_shape=jax.ShapeDtypeStruct((batch, rows, n), BF16),
        scratch_shapes=[pltpu.VMEM((tm, d), BF16)],
        compiler_params=_params(("parallel", "parallel", "arbitrary")),
        name="q_proj",
    )(h, g, w_q, g_q)


def _sb_sweep(q_ref, k_ref, v_ref, u, acc_scr, car_scr, start, mask):
    d = SB_HEAD_DIM
    heads = [slice(hd * d, (hd + 1) * d) for hd in range(SB_HG)]
    zs = [_dot_nt(q_ref[0, :, hs], k_ref[0, pl.ds(start, d), hs]) for hs in heads]
    cats, logits = [], []
    for z in zs:
        lf = _log_sigmoid(-z)
        if mask is not None:
            lf = jnp.where(mask, lf, 0.0)
        hi, lo = _split_bf16(lf)
        cats.append(jnp.concatenate([hi, lo], axis=1))
        logits.append(lf + z)
    sums = [_dot(c, u) for c in cats]
    atts = []
    alive = None
    for hd in range(SB_HG):
        carry = car_scr[hd]
        att = jnp.exp(logits[hd] + sums[hd][:, :d] + carry)
        if mask is not None:
            att = jnp.where(mask, att, 0.0)
        atts.append(att.astype(BF16))
        carry = carry + sums[hd][:, d:]
        car_scr[hd] = carry
        alive = carry if alive is None else jnp.maximum(alive, carry)
    for hd, hs in enumerate(heads):
        acc_scr[hd] += _dot(atts[hd], v_ref[0, pl.ds(start, d), hs])
    return jnp.max(alive)


def _sb_kernel(q_ref, k_ref, v_ref, u_ref, o_ref, acc_scr, car_scr, *, lead_start):
    i = pl.program_id(2)
    d = SB_HEAD_DIM
    acc_scr[...] = jnp.zeros_like(acc_scr)
    car_scr[...] = jnp.zeros_like(car_scr)
    row = lax.broadcasted_iota(jnp.int32, (d, d), 0)
    col = lax.broadcasted_iota(jnp.int32, (d, d), 1)
    u = u_ref[...]
    sweep = functools.partial(_sb_sweep, q_ref, k_ref, v_ref, u, acc_scr, car_scr)

    alive0 = sweep(pl.multiple_of(i * d, d), col < row)

    def cond(state):
        jb, alive = state
        return jnp.logical_and(jb >= 0, alive > SB_DEAD_LOG)

    def body(state):
        jb, _ = state
        return jb - 1, sweep(pl.multiple_of(jb * d, d), None)

    jb, alive = lax.while_loop(cond, body, (i - 1, alive0))

    @pl.when(jnp.logical_and(jb < 0, alive > SB_DEAD_LOG))
    def _():
        sweep(lead_start, col >= PAD)

    for hd in range(SB_HG):
        o_ref[0, :, hd * d:(hd + 1) * d] = acc_scr[hd].astype(o_ref.dtype)


def _sb_attn(q, kv, u):
    batch, rows, dm = q.shape
    length = kv.shape[1]
    gw = SB_HG * SB_HEAD_DIM
    return pl.pallas_call(
        functools.partial(_sb_kernel, lead_start=rows),
        grid=(batch, dm // gw, rows // SB_TQ),
        in_specs=[
            pl.BlockSpec((1, SB_TQ, gw), lambda b, h, i: (b, i, h)),
            pl.BlockSpec((1, length, gw), lambda b, h, i: (b, 0, h)),
            pl.BlockSpec((1, length, gw), lambda b, h, i: (b, 0, h + dm // gw)),
            pl.BlockSpec((2 * SB_HEAD_DIM, 2 * SB_HEAD_DIM), lambda b, h, i: (0, 0)),
        ],
        out_specs=pl.BlockSpec((1, SB_TQ, gw), lambda b, h, i: (b, i, h)),
        out_shape=jax.ShapeDtypeStruct((batch, rows, dm), BF16),
        scratch_shapes=[
            pltpu.VMEM((SB_HG, SB_TQ, SB_HEAD_DIM), F32),
            pltpu.VMEM((SB_HG, SB_TQ, SB_HEAD_DIM), F32),
        ],
        compiler_params=_params(("parallel", "parallel", "arbitrary")),
        name="sb_attn",
    )(q, kv, kv, u)


def _suffix_matrix():
    r = jnp.arange(SB_HEAD_DIM)
    later = (r[:, None] > r[None, :]).astype(BF16)
    half = jnp.concatenate([later, jnp.ones((SB_HEAD_DIM, SB_HEAD_DIM), BF16)], axis=1)
    return jnp.concatenate([half, half], axis=0)


def kernel(x, meta_tokens, g_norm_a, w_in_a, w_gate_up_a, b_gate_a, g_onorm_a, w_out_a, g_kv_norm, w_kv, g_k, g_norm_b, w_q_b, g_q_b, w_o_b, g_ffn_norm, w_ffn_gate, w_ffn_up, w_ffn_down):
    batch, seq, d = x.shape
    length = seq + LEAD
    n_a = g_norm_a.shape[0]
    n_b = g_norm_b.shape[0]
    n_main = w_in_a.shape[-1] - GATE_RANK

    lead = jnp.concatenate([jnp.zeros((PAD, d), x.dtype), meta_tokens.astype(x.dtype)], axis=0)
    h = jnp.concatenate([x, jnp.broadcast_to(lead[None], (batch, LEAD, d))], axis=1)

    def row(v):
        return v.reshape(1, -1).astype(F32)

    w_in_bf = w_in_a.astype(BF16)
    w_gate_bf = w_ffn_gate.astype(BF16)
    w_up_bf = w_ffn_up.astype(BF16)
    w_down_bf = w_ffn_down.astype(BF16)

    for layer in range(n_a):
        w_a = jnp.pad(w_in_bf[layer][:, n_main:], ((0, 0), (0, LANES - GATE_RANK)))
        w_gu = jnp.pad(w_gate_up_a[layer], ((0, LANES - GATE_RANK), (0, 0))).astype(BF16)
        proj, la = _gla_proj(h.reshape(batch * length, d), row(g_norm_a[layer]), w_in_bf, layer,
                             w_a, w_gu, row(b_gate_a[layer]), tm=1056, tn=1024)
        o = _gla_core(proj, la, row(g_onorm_a[layer]), batch, length)
        h = _proj_res(o, w_out_a[layer].astype(BF16), h, tm=1056, tn=1024)
        h = _ffn(h, row(g_ffn_norm[layer]), w_gate_bf, w_up_bf, w_down_bf, layer,
                 rows=length, tm=704, tf=512)

    kv = _kv_proj(h, row(g_kv_norm), w_kv.astype(BF16), row(g_k), tm=1056, tn=1024)
    u = _suffix_matrix()
    for j in range(n_b):
        layer = n_a + j
        q = _q_proj(h, row(g_norm_b[j]), w_q_b[j].astype(BF16), row(g_q_b[j]),
                    rows=seq, tm=1024, tn=1024)
        o = _sb_attn(q, kv, u)
        h = _proj_res(o, w_o_b[j].astype(BF16), h, tm=1024, tn=1024)
        h = _ffn(h, row(g_ffn_norm[layer]), w_gate_bf, w_up_bf, w_down_bf, layer,
                 rows=seq, tm=512, tf=512)
    return h
```

```python
import functools
from typing import NamedTuple

import jax
import jax.numpy as jnp
from jax import lax
from jax.experimental import pallas as pl
from jax.experimental.pallas import tpu as pltpu

F32 = jnp.float32
BF16 = jnp.bfloat16

CHUNK = 64
N_META = 16
LEAD = 128
PAD = LEAD - N_META
GLA_HEADS = 4
GATE_RANK = 16
GATE_TAU = 16.0
SB_HEAD_DIM = 128
EPS = 1e-6

LANES = 128
VMEM_LIMIT = 56 * 1024 * 1024

SB_DEAD_LOG = -88.0
SB_TQ = SB_HEAD_DIM
SB_HG = 8


def _params(sem):
    return pltpu.CompilerParams(dimension_semantics=sem, vmem_limit_bytes=VMEM_LIMIT)


def _dot(a, b):
    return jnp.dot(a, b, preferred_element_type=F32)


def _dot_nt(a, b):
    return lax.dot_general(a, b, (((1,), (1,)), ((), ())), preferred_element_type=F32)


def _dot_tn(a, b):
    return lax.dot_general(a, b, (((0,), (0,)), ((), ())), preferred_element_type=F32)


def _rms(x, g):
    ms = jnp.mean(x * x, axis=-1, keepdims=True)
    return x * lax.rsqrt(ms + EPS) * g


def _log_sigmoid(x):
    return jnp.minimum(x, 0.0) - jnp.log(1.0 + jnp.exp(-jnp.abs(x)))


def _split_bf16(x):
    hi = x.astype(BF16)
    lo = (x - hi.astype(F32)).astype(BF16)
    return hi, lo


class _Cast(NamedTuple):
    src: jax.Array
    in_spec: pl.BlockSpec
    out_spec: pl.BlockSpec
    out_shape: jax.ShapeDtypeStruct


def _cast_job(w, layer, block, lin, n_steps):
    rows, cols = w.shape[-2:]
    br, bc = block
    n_c = cols // bc
    n_blocks = (rows // br) * n_c
    assert rows % br == 0 and cols % bc == 0 and n_blocks <= n_steps

    def idx(*g):
        blk = lin(*g) * n_blocks // n_steps
        return blk // n_c, blk % n_c

    if w.ndim == 3:
        in_spec = pl.BlockSpec((None, br, bc), lambda *g: (layer, *idx(*g)))
    else:
        in_spec = pl.BlockSpec((br, bc), idx)
    return _Cast(w, in_spec, pl.BlockSpec((br, bc), idx), jax.ShapeDtypeStruct((rows, cols), BF16))


def _hosting(body, n_in, n_out, n_cast):
    def kernel(*refs):
        a, b, c = n_in, n_in + n_cast, n_in + n_cast + n_out

        def side():
            for src, dst in zip(refs[a:b], refs[c:c + n_cast]):
                dst[...] = src[...].astype(dst.dtype)

        body(*refs[:a], *refs[b:c], *refs[c + n_cast:], side=side)
    return kernel


def _host_call(body, casts, in_specs, out_specs, out_shape, args, **kwargs):
    n_out = len(out_specs)
    outs = pl.pallas_call(
        _hosting(body, len(in_specs), n_out, len(casts)),
        in_specs=list(in_specs) + [c.in_spec for c in casts],
        out_specs=list(out_specs) + [c.out_spec for c in casts],
        out_shape=list(out_shape) + [c.out_shape for c in casts],
        **kwargs,
    )(*args, *[c.src for c in casts])
    return outs[:n_out], outs[n_out:]


def _head_rms(x, g, scale):
    outs = []
    for h in range(x.shape[1] // LANES):
        xs = x[:, h * LANES:(h + 1) * LANES]
        ms = jnp.mean(xs * xs, axis=-1, keepdims=True)
        outs.append(xs * lax.rsqrt(ms + EPS) * g * scale)
    return outs


def _gla_proj_kernel(h_ref, g_ref, w_ref, wa_ref, wgu_ref, bg_ref, proj_ref, la_ref, y_scr):
    @pl.when(pl.program_id(1) == 0)
    def _():
        y = _rms(h_ref[...], g_ref[...]).astype(BF16)
        y_scr[...] = y
        a = _dot(y, wa_ref[...])
        z = _dot(a.astype(BF16), wgu_ref[...]) + bg_ref[...]
        la_ref[...] = _log_sigmoid(z) * (1.0 / GATE_TAU)

    proj_ref[...] = _dot(y_scr[...], w_ref[...])


def _gla_proj(h, g, w_in, layer, w_a, w_gu, b_gate, tm, tn):
    m, d = h.shape
    n = w_in.shape[-1] - GATE_RANK
    kd = w_gu.shape[1]
    return pl.pallas_call(
        _gla_proj_kernel,
        grid=(m // tm, n // tn),
        in_specs=[
            pl.BlockSpec((tm, d), lambda i, j: (i, 0)),
            pl.BlockSpec((1, d), lambda i, j: (0, 0)),
            pl.BlockSpec((None, d, tn), lambda i, j: (layer, 0, j)),
            pl.BlockSpec((d, LANES), lambda i, j: (0, 0)),
            pl.BlockSpec((LANES, kd), lambda i, j: (0, 0)),
            pl.BlockSpec((1, kd), lambda i, j: (0, 0)),
        ],
        out_specs=[
            pl.BlockSpec((tm, tn), lambda i, j: (i, j)),
            pl.BlockSpec((tm, kd), lambda i, j: (i, 0)),
        ],
        out_shape=[
            jax.ShapeDtypeStruct((m, n), F32),
            jax.ShapeDtypeStruct((m, kd), F32),
        ],
        scratch_shapes=[pltpu.VMEM((tm, d), BF16)],
        compiler_params=_params(("parallel", "arbitrary")),
        name="gla_proj",
    )(h, g, w_in, w_a, w_gu, b_gate)


def _gla_core_kernel(q_ref, k_ref, v_ref, g_ref, la_ref, gon_ref, o_ref, st_scr, *, dk, dv, side):
    c = pl.program_id(1)

    @pl.when(c == 0)
    def _():
        st_scr[...] = jnp.zeros_like(st_scr)

    side()

    row = lax.broadcasted_iota(jnp.int32, (CHUNK, CHUNK), 0)
    col = lax.broadcasted_iota(jnp.int32, (CHUNK, CHUNK), 1)
    tri = (col <= row).astype(BF16)
    lower = row >= col
    pos = c * CHUNK + lax.broadcasted_iota(jnp.int32, (CHUNK, 1), 0)
    key_ok = pos >= PAD

    la_hi, la_lo = _split_bf16(la_ref[0])
    bcum_all = _dot(tri, la_hi) + _dot(tri, la_lo)
    gon = gon_ref[...]

    for hd in range(GLA_HEADS):
        ks = slice(hd * dk, (hd + 1) * dk)
        vs = slice(hd * dv, (hd + 1) * dv)
        q = q_ref[0, :, ks] * (dk ** -0.5)
        k = jnp.where(key_ok, k_ref[0, :, ks], 0.0)
        v = v_ref[0, :, vs].astype(BF16)
        b = bcum_all[:, ks]
        b_ref = b[CHUNK // 2 - 1:CHUNK // 2, :]
        b_last = b[CHUNK - 1:CHUNK, :]
        e_pos = jnp.exp(b - b_ref)
        e_neg = jnp.exp(b_ref - b)
        s_lo = _dot_nt((q * e_pos).astype(BF16), (k * e_neg).astype(BF16))
        s_up = _dot_nt((q * e_neg).astype(BF16), (k * e_pos).astype(BF16))
        scores = jnp.where(lower, s_lo, s_up).astype(BF16)
        st = st_scr[hd]
        o = _dot(scores, v) + _dot_nt((q * jnp.exp(b)).astype(BF16), st.astype(BF16))
        k_state = (k * jnp.exp(b_last - b)).astype(BF16)
        st_scr[hd] = st * jnp.exp(b_last) + _dot_tn(v, k_state)
        o = _rms(o, gon)
        gate = g_ref[0, :, vs]
        o_ref[0, :, vs] = (o * (gate * jax.nn.sigmoid(gate))).astype(o_ref.dtype)


def _gla_core(proj, la, g_onorm, batch, length, make_casts):
    kd = la.shape[-1]
    vd = (proj.shape[-1] - 2 * kd) // 2
    dk, dv = kd // GLA_HEADS, vd // GLA_HEADS
    n_c = length // CHUNK
    shift = n_c - LEAD // CHUNK
    proj = proj.reshape(batch, length, proj.shape[-1])
    la = la.reshape(batch, length, kd)

    def blk(col):
        return lambda b, c: (b, (c + shift) % n_c, col)

    (o,), cast = _host_call(
        functools.partial(_gla_core_kernel, dk=dk, dv=dv),
        make_casts(lambda b, c: b * n_c + c, batch * n_c),
        in_specs=[
            pl.BlockSpec((1, CHUNK, kd), blk(0)),
            pl.BlockSpec((1, CHUNK, kd), blk(1)),
            pl.BlockSpec((1, CHUNK, vd), blk(kd * 2 // vd)),
            pl.BlockSpec((1, CHUNK, vd), blk(kd * 2 // vd + 1)),
            pl.BlockSpec((1, CHUNK, kd), blk(0)),
            pl.BlockSpec((1, dv), lambda b, c: (0, 0)),
        ],
        out_specs=[pl.BlockSpec((1, CHUNK, vd), blk(0))],
        out_shape=[jax.ShapeDtypeStruct((batch, length, vd), BF16)],
        args=(proj, proj, proj, proj, la, g_onorm),
        grid=(batch, n_c),
        scratch_shapes=[pltpu.VMEM((GLA_HEADS, dv, dk), F32)],
        compiler_params=_params(("arbitrary", "arbitrary")),
        name="gla_core",
    )
    return o, cast


def _proj_res_kernel(x_ref, w_ref, res_ref, o_ref, *, side):
    side()
    o_ref[0] = res_ref[0] + _dot(x_ref[0], w_ref[...])


def _proj_res(x, w, res, tm, tn, make_casts=lambda lin, n_steps: ()):
    batch, rows, d = x.shape
    n = w.shape[1]
    n_i, n_j = rows // tm, n // tn
    (o,), cast = _host_call(
        _proj_res_kernel,
        make_casts(lambda b, i, j: (b * n_i + i) * n_j + j, batch * n_i * n_j),
        in_specs=[
            pl.BlockSpec((1, tm, d), lambda b, i, j: (b, i, 0)),
            pl.BlockSpec((d, tn), lambda b, i, j: (0, j)),
            pl.BlockSpec((1, tm, tn), lambda b, i, j: (b, i, j)),
        ],
        out_specs=[pl.BlockSpec((1, tm, tn), lambda b, i, j: (b, i, j))],
        out_shape=[jax.ShapeDtypeStruct((batch, rows, n), F32)],
        args=(x, w, res),
        grid=(batch, n_i, n_j),
        compiler_params=_params(("arbitrary", "arbitrary", "arbitrary")),
        name="proj_res",
    )
    return o, cast


def _ffn_kernel(h_ref, g_ref, wg_ref, wu_ref, wd_ref, o_ref, y_scr, *, side):
    @pl.when(pl.program_id(2) == 0)
    def _():
        h = h_ref[0]
        y_scr[...] = _rms(h, g_ref[...]).astype(BF16)
        o_ref[0] = h

    side()
    y = y_scr[...]
    gate = _dot(y, wg_ref[...])
    up = _dot(y, wu_ref[...])
    act = (gate * jax.nn.sigmoid(gate) * up).astype(BF16)
    o_ref[0] += _dot(act, wd_ref[...])


def _ffn(h, g, w_gate, w_up, w_down, rows, tm, tf, make_casts=lambda lin, n_steps: ()):
    batch, _, d = h.shape
    d_ff = w_gate.shape[-1]
    n_i, n_f = rows // tm, d_ff // tf
    (o,), cast = _host_call(
        _ffn_kernel,
        make_casts(lambda b, i, f: (b * n_i + i) * n_f + f, batch * n_i * n_f),
        in_specs=[
            pl.BlockSpec((1, tm, d), lambda b, i, f: (b, i, 0)),
            pl.BlockSpec((1, d), lambda b, i, f: (0, 0)),
            pl.BlockSpec((d, tf), lambda b, i, f: (0, f)),
            pl.BlockSpec((d, tf), lambda b, i, f: (0, f)),
            pl.BlockSpec((tf, d), lambda b, i, f: (f, 0)),
        ],
        out_specs=[pl.BlockSpec((1, tm, d), lambda b, i, f: (b, i, 0))],
        out_shape=[jax.ShapeDtypeStruct((batch, rows, d), F32)],
        args=(h, g, w_gate, w_up, w_down),
        grid=(batch, n_i, n_f),
        scratch_shapes=[pltpu.VMEM((tm, d), BF16)],
        compiler_params=_params(("arbitrary", "arbitrary", "arbitrary")),
        name="ffn",
    )
    return o, cast


def _kv_proj_kernel(h_ref, g_ref, w_ref, gk_ref, o_ref, y_scr, *, n_key_tiles):
    j = pl.program_id(2)

    @pl.when(j == 0)
    def _():
        y_scr[...] = _rms(h_ref[0], g_ref[...]).astype(BF16)

    r = _dot(y_scr[...], w_ref[...])
    normed = jnp.concatenate(_head_rms(r, gk_ref[...], 1.0), axis=1)
    o_ref[0] = jnp.where(j < n_key_tiles, normed, r).astype(o_ref.dtype)


def _kv_proj(h, g, w_kv, g_k, tm, tn):
    batch, rows, d = h.shape
    n = w_kv.shape[1]
    return pl.pallas_call(
        functools.partial(_kv_proj_kernel, n_key_tiles=(n // 2) // tn),
        grid=(batch, rows // tm, n // tn),
        in_specs=[
            pl.BlockSpec((1, tm, d), lambda b, i, j: (b, i, 0)),
            pl.BlockSpec((1, d), lambda b, i, j: (0, 0)),
            pl.BlockSpec((d, tn), lambda b, i, j: (0, j)),
            pl.BlockSpec((1, SB_HEAD_DIM), lambda b, i, j: (0, 0)),
        ],
        out_specs=pl.BlockSpec((1, tm, tn), lambda b, i, j: (b, i, j)),
        out_shape=jax.ShapeDtypeStruct((batch, rows, n), BF16),
        scratch_shapes=[pltpu.VMEM((tm, d), BF16)],
        compiler_params=_params(("parallel", "parallel", "arbitrary")),
        name="kv_proj",
    )(h, g, w_kv, g_k)


def _q_proj_kernel(h_ref, g_ref, w_ref, gq_ref, o_ref, y_scr):
    @pl.when(pl.program_id(2) == 0)
    def _():
        y_scr[...] = _rms(h_ref[0], g_ref[...]).astype(BF16)

    r = _dot(y_scr[...], w_ref[...])
    heads = _head_rms(r, gq_ref[...], SB_HEAD_DIM ** -0.5)
    o_ref[0] = jnp.concatenate(heads, axis=1).astype(o_ref.dtype)


def _q_proj(h, g, w_q, g_q, rows, tm, tn):
    batch, _, d = h.shape
    n = w_q.shape[1]
    return pl.pallas_call(
        _q_proj_kernel,
        grid=(batch, rows // tm, n // tn),
        in_specs=[
            pl.BlockSpec((1, tm, d), lambda b, i, j: (b, i, 0)),
            pl.BlockSpec((1, d), lambda b, i, j: (0, 0)),
            pl.BlockSpec((d, tn), lambda b, i, j: (0, j)),
            pl.BlockSpec((1, SB_HEAD_DIM), lambda b, i, j: (0, 0)),
        ],
        out_specs=pl.BlockSpec((1, tm, tn), lambda b, i, j: (b, i, j)),
        out_shape=jax.ShapeDtypeStruct((batch, rows, n), BF16),
        scratch_shapes=[pltpu.VMEM((tm, d), BF16)],
        compiler_params=_params(("parallel", "parallel", "arbitrary")),
        name="q_proj",
    )(h, g, w_q, g_q)


def _sb_scores(q_ref, k_ref, z_scr, start):
    d = SB_HEAD_DIM
    for hd in range(SB_HG):
        hs = slice(hd * d, (hd + 1) * d)
        z_scr[hd] = _dot_nt(q_ref[0, :, hs], k_ref[0, pl.ds(start, d), hs])


def _sb_sweep(q_ref, k_ref, v_ref, u, acc_scr, car_scr, z_scr, start, next_start, mask):
    d = SB_HEAD_DIM
    cats, logits = [], []
    for hd in range(SB_HG):
        z = z_scr[hd]
        lf = _log_sigmoid(-z)
        if mask is not None:
            lf = jnp.where(mask, lf, 0.0)
        hi, lo = _split_bf16(lf)
        cats.append(jnp.concatenate([hi, lo], axis=1))
        logits.append(lf + z)
    sums = [_dot(c, u) for c in cats]
    if next_start is not None:
        _sb_scores(q_ref, k_ref, z_scr, next_start)
    atts = []
    alive = None
    for hd in range(SB_HG):
        carry = car_scr[hd]
        att = jnp.exp(logits[hd] + sums[hd][:, :d] + carry)
        if mask is not None:
            att = jnp.where(mask, att, 0.0)
        atts.append(att.astype(BF16))
        carry = carry + sums[hd][:, d:]
        car_scr[hd] = carry
        alive = carry if alive is None else jnp.maximum(alive, carry)
    for hd in range(SB_HG):
        acc_scr[hd] += _dot(atts[hd], v_ref[0, pl.ds(start, d), hd * d:(hd + 1) * d])
    return jnp.max(alive)


def _sb_kernel(q_ref, k_ref, v_ref, u_ref, o_ref, acc_scr, car_scr, z_scr, *, lead_start):
    i = pl.program_id(2)
    d = SB_HEAD_DIM
    acc_scr[...] = jnp.zeros_like(acc_scr)
    car_scr[...] = jnp.zeros_like(car_scr)
    row = lax.broadcasted_iota(jnp.int32, (d, d), 0)
    col = lax.broadcasted_iota(jnp.int32, (d, d), 1)
    u = u_ref[...]
    sweep = functools.partial(_sb_sweep, q_ref, k_ref, v_ref, u, acc_scr, car_scr, z_scr)

    def block_start(jb):
        return pl.multiple_of(jnp.where(jb >= 0, jb * d, lead_start), d)

    _sb_scores(q_ref, k_ref, z_scr, block_start(i))
    alive0 = sweep(block_start(i), block_start(i - 1), col < row)

    def cond(state):
        jb, alive = state
        return jnp.logical_and(jb >= 0, alive > SB_DEAD_LOG)

    def body(state):
        jb, _ = state
        return jb - 1, sweep(block_start(jb), block_start(jb - 1), None)

    jb, alive = lax.while_loop(cond, body, (i - 1, alive0))

    @pl.when(jnp.logical_and(jb < 0, alive > SB_DEAD_LOG))
    def _():
        sweep(lead_start, None, col >= PAD)

    for hd in range(SB_HG):
        o_ref[0, :, hd * d:(hd + 1) * d] = acc_scr[hd].astype(o_ref.dtype)


def _sb_attn(q, kv, u):
    batch, rows, dm = q.shape
    length = kv.shape[1]
    gw = SB_HG * SB_HEAD_DIM
    return pl.pallas_call(
        functools.partial(_sb_kernel, lead_start=rows),
        grid=(batch, dm // gw, rows // SB_TQ),
        in_specs=[
            pl.BlockSpec((1, SB_TQ, gw), lambda b, h, i: (b, i, h)),
            pl.BlockSpec((1, length, gw), lambda b, h, i: (b, 0, h)),
            pl.BlockSpec((1, length, gw), lambda b, h, i: (b, 0, h + dm // gw)),
            pl.BlockSpec((2 * SB_HEAD_DIM, 2 * SB_HEAD_DIM), lambda b, h, i: (0, 0)),
        ],
        out_specs=pl.BlockSpec((1, SB_TQ, gw), lambda b, h, i: (b, i, h)),
        out_shape=jax.ShapeDtypeStruct((batch, rows, dm), BF16),
        scratch_shapes=[pltpu.VMEM((SB_HG, SB_TQ, SB_HEAD_DIM), F32)] * 3,
        compiler_params=_params(("parallel", "parallel", "arbitrary")),
        name="sb_attn",
    )(q, kv, kv, u)


def _suffix_matrix():
    r = jnp.arange(SB_HEAD_DIM)
    later = (r[:, None] > r[None, :]).astype(BF16)
    half = jnp.concatenate([later, jnp.ones((SB_HEAD_DIM, SB_HEAD_DIM), BF16)], axis=1)
    return jnp.concatenate([half, half], axis=0)


def kernel(x, meta_tokens, g_norm_a, w_in_a, w_gate_up_a, b_gate_a, g_onorm_a, w_out_a, g_kv_norm, w_kv, g_k, g_norm_b, w_q_b, g_q_b, w_o_b, g_ffn_norm, w_ffn_gate, w_ffn_up, w_ffn_down):
    batch, seq, d = x.shape
    length = seq + LEAD
    n_a = g_norm_a.shape[0]
    n_b = g_norm_b.shape[0]
    n_main = w_in_a.shape[-1] - GATE_RANK

    lead = jnp.concatenate([jnp.zeros((PAD, d), x.dtype), meta_tokens.astype(x.dtype)], axis=0)
    h = jnp.concatenate([x, jnp.broadcast_to(lead[None], (batch, LEAD, d))], axis=1)

    def row(v):
        return v.reshape(1, -1).astype(F32)

    ready = {}

    def bf(name, w, layer=None):
        if (name, layer) in ready:
            return ready.pop((name, layer))
        return (w if layer is None else w[layer]).astype(BF16)

    def plan(jobs):
        def make_casts(lin, n_steps):
            return [_cast_job(w, layer, block, lin, n_steps) for _, w, layer, block in jobs]

        def collect(converted):
            for (name, _, layer, _), c in zip(jobs, converted):
                ready[(name, layer)] = c
        return make_casts, collect

    def ffn_jobs(layer):
        return [("gate", w_ffn_gate, layer, (1024, 128)), ("up", w_ffn_up, layer, (1024, 128)),
                ("down", w_ffn_down, layer, (128, 1024))]

    def ffn_weights(layer):
        return (bf("gate", w_ffn_gate, layer), bf("up", w_ffn_up, layer),
                bf("down", w_ffn_down, layer))

    w_in_bf = w_in_a.astype(BF16)
    for layer in range(n_a):
        last_a = layer == n_a - 1
        w_a = jnp.pad(w_in_bf[layer][:, n_main:], ((0, 0), (0, LANES - GATE_RANK)))
        w_gu = jnp.pad(w_gate_up_a[layer], ((0, LANES - GATE_RANK), (0, 0))).astype(BF16)
        proj, la = _gla_proj(h.reshape(batch * length, d), row(g_norm_a[layer]), w_in_bf, layer,
                             w_a, w_gu, row(b_gate_a[layer]), tm=1056, tn=1024)
        make_casts, collect = plan([("out", w_out_a, layer, (128, 1024))] + ffn_jobs(layer))
        o, converted = _gla_core(proj, la, row(g_onorm_a[layer]), batch, length, make_casts)
        collect(converted)
        make_casts, collect = plan([("kv", w_kv, None, (128, w_kv.shape[1]))] if last_a else [])
        h, converted = _proj_res(o, bf("out", w_out_a, layer), h, tm=1056, tn=1024,
                                 make_casts=make_casts)
        collect(converted)
        jobs = []
        if last_a and n_b:
            jobs = [("q", w_q_b, 0, (128, 1024)), ("o", w_o_b, 0, (128, 1024))] + ffn_jobs(n_a)
        make_casts, collect = plan(jobs)
        h, converted = _ffn(h, row(g_ffn_norm[layer]), *ffn_weights(layer),
                            rows=length, tm=704, tf=512, make_casts=make_casts)
        collect(converted)

    kv = _kv_proj(h, row(g_kv_norm), bf("kv", w_kv), row(g_k), tm=1056, tn=1024)
    u = _suffix_matrix()
    for j in range(n_b):
        layer = n_a + j
        q = _q_proj(h, row(g_norm_b[j]), bf("q", w_q_b, j), row(g_q_b[j]),
                    rows=seq, tm=1024, tn=1024)
        o = _sb_attn(q, kv, u)
        h, _ = _proj_res(o, bf("o", w_o_b, j), h, tm=1024, tn=1024)
        jobs = []
        if j + 1 < n_b:
            jobs = ([("q", w_q_b, j + 1, (128, 1024)), ("o", w_o_b, j + 1, (128, 1024))]
                    + ffn_jobs(layer + 1))
        make_casts, collect = plan(jobs)
        h, converted = _ffn(h, row(g_ffn_norm[layer]), *ffn_weights(layer),
                            rows=seq, tm=512, tf=512, make_casts=make_casts)
        collect(converted)
    return h
```

```python
import functools
from typing import NamedTuple

import jax
import jax.numpy as jnp
from jax import lax
from jax.experimental import pallas as pl
from jax.experimental.pallas import tpu as pltpu

F32 = jnp.float32
BF16 = jnp.bfloat16

CHUNK = 64
N_META = 16
LEAD = 128
PAD = LEAD - N_META
GLA_HEADS = 4
GLA_BLOCK = 2 * CHUNK
GATE_RANK = 16
GATE_TAU = 16.0
SB_HEAD_DIM = 128
EPS = 1e-6

LANES = 128
VMEM_LIMIT = 56 * 1024 * 1024

SB_DEAD_LOG = -88.0
SB_TQ = SB_HEAD_DIM
SB_HG = 8


def _params(sem):
    return pltpu.CompilerParams(dimension_semantics=sem, vmem_limit_bytes=VMEM_LIMIT)


def _dot(a, b):
    return jnp.dot(a, b, preferred_element_type=F32)


def _dot_nt(a, b):
    return lax.dot_general(a, b, (((1,), (1,)), ((), ())), preferred_element_type=F32)


def _dot_tn(a, b):
    return lax.dot_general(a, b, (((0,), (0,)), ((), ())), preferred_element_type=F32)


def _rms(x, g):
    ms = jnp.mean(x * x, axis=-1, keepdims=True)
    return x * lax.rsqrt(ms + EPS) * g


def _log_sigmoid(x):
    return jnp.minimum(x, 0.0) - jnp.log(1.0 + jnp.exp(-jnp.abs(x)))


def _split_bf16(x):
    hi = x.astype(BF16)
    lo = (x - hi.astype(F32)).astype(BF16)
    return hi, lo


_PROBE_ROWS = 16


class _Cast(NamedTuple):
    src: jax.Array
    in_spec: pl.BlockSpec
    out_spec: pl.BlockSpec
    out_shape: jax.ShapeDtypeStruct


def _cast_job(w, layer, block, lin, n_steps):
    rows, cols = w.shape[-2:]
    br, bc = block
    n_c = cols // bc
    n_blocks = (rows // br) * n_c
    assert rows % br == 0 and cols % bc == 0 and n_blocks <= n_steps
    assert br % _PROBE_ROWS == 0 and br >= 2 * _PROBE_ROWS

    def idx(*g):
        blk = lin(*g) * n_blocks // n_steps
        return blk // n_c, blk % n_c

    if w.ndim == 3:
        in_spec = pl.BlockSpec((None, br, bc), lambda *g: (layer, *idx(*g)))
    else:
        in_spec = pl.BlockSpec((br, bc), idx)
    return _Cast(w, in_spec, pl.BlockSpec((br, bc), idx), jax.ShapeDtypeStruct((rows, cols), BF16))


def _hosting(body, n_in, n_out, n_cast):
    def kernel(*refs):
        a, b, c = n_in, n_in + n_cast, n_in + n_cast + n_out

        def side():
            if not n_cast:
                return None
            off = pl.multiple_of((pl.program_id(0) % 2) * _PROBE_ROWS, _PROBE_ROWS)
            bits = None
            for src, dst in zip(refs[a:b], refs[c:c + n_cast]):
                dst[...] = src[...].astype(dst.dtype)
                probe = pltpu.bitcast(dst[pl.ds(off, _PROBE_ROWS), :], jnp.uint32)
                for l in range(0, probe.shape[1], LANES):
                    piece = probe[:, l:l + LANES]
                    bits = piece if bits is None else bits | piece
            zero = pltpu.bitcast((bits >> 16) >> 16, F32)
            return jnp.max(zero, axis=(0, 1), keepdims=True)

        body(*refs[:a], *refs[b:c], *refs[c + n_cast:], side=side)
    return kernel


def _host_call(body, casts, in_specs, out_specs, out_shape, args, **kwargs):
    n_out = len(out_specs)
    outs = pl.pallas_call(
        _hosting(body, len(in_specs), n_out, len(casts)),
        in_specs=list(in_specs) + [c.in_spec for c in casts],
        out_specs=list(out_specs) + [c.out_spec for c in casts],
        out_shape=list(out_shape) + [c.out_shape for c in casts],
        **kwargs,
    )(*args, *[c.src for c in casts])
    return outs[:n_out], outs[n_out:]


def _head_rms(x, g, scale):
    outs = []
    for h in range(x.shape[1] // LANES):
        xs = x[:, h * LANES:(h + 1) * LANES]
        ms = jnp.mean(xs * xs, axis=-1, keepdims=True)
        outs.append(xs * lax.rsqrt(ms + EPS) * g * scale)
    return outs


def _gla_proj_kernel(h_ref, g_ref, w_ref, wa_ref, wgu_ref, bg_ref, proj_ref, la_ref, y_scr):
    @pl.when(pl.program_id(1) == 0)
    def _():
        y = _rms(h_ref[...], g_ref[...]).astype(BF16)
        y_scr[...] = y
        a = _dot(y, wa_ref[...])
        z = _dot(a.astype(BF16), wgu_ref[...]) + bg_ref[...]
        la_ref[...] = _log_sigmoid(z) * (1.0 / GATE_TAU)

    proj_ref[...] = _dot(y_scr[...], w_ref[...])


def _gla_proj(h, g, w_in, layer, w_a, w_gu, b_gate, tm, tn):
    m, d = h.shape
    n = w_in.shape[-1] - GATE_RANK
    kd = w_gu.shape[1]
    return pl.pallas_call(
        _gla_proj_kernel,
        grid=(m // tm, n // tn),
        in_specs=[
            pl.BlockSpec((tm, d), lambda i, j: (i, 0)),
            pl.BlockSpec((1, d), lambda i, j: (0, 0)),
            pl.BlockSpec((None, d, tn), lambda i, j: (layer, 0, j)),
            pl.BlockSpec((d, LANES), lambda i, j: (0, 0)),
            pl.BlockSpec((LANES, kd), lambda i, j: (0, 0)),
            pl.BlockSpec((1, kd), lambda i, j: (0, 0)),
        ],
        out_specs=[
            pl.BlockSpec((tm, tn), lambda i, j: (i, j)),
            pl.BlockSpec((tm, kd), lambda i, j: (i, 0)),
        ],
        out_shape=[
            jax.ShapeDtypeStruct((m, n), F32),
            jax.ShapeDtypeStruct((m, kd), F32),
        ],
        scratch_shapes=[pltpu.VMEM((tm, d), BF16)],
        compiler_params=_params(("parallel", "arbitrary")),
        name="gla_proj",
    )(h, g, w_in, w_a, w_gu, b_gate)


def _gla_core_kernel(q_ref, k_ref, v_ref, g_ref, la_ref, gon_ref, o_ref, st_scr, *, dk, dv, side):
    blk = pl.program_id(1)
    rows = q_ref.shape[1]

    @pl.when(blk == 0)
    def _():
        st_scr[...] = jnp.zeros_like(st_scr)

    zero = side()

    row = lax.broadcasted_iota(jnp.int32, (rows, rows), 0)
    col = lax.broadcasted_iota(jnp.int32, (rows, rows), 1)
    same_chunk = (row ^ col) < CHUNK
    tri = jnp.logical_and(col <= row, same_chunk).astype(BF16)
    lower = (lax.broadcasted_iota(jnp.int32, (CHUNK, CHUNK), 0)
             >= lax.broadcasted_iota(jnp.int32, (CHUNK, CHUNK), 1))
    chunk_pos = blk * rows + lax.broadcasted_iota(jnp.int32, (CHUNK, 1), 0)

    la_hi, la_lo = _split_bf16(la_ref[0])
    bcum_all = _dot(tri, la_hi) + _dot(tri, la_lo)
    gon = gon_ref[...]

    for sub in range(rows // CHUNK):
        rs = slice(sub * CHUNK, (sub + 1) * CHUNK)
        key_ok = chunk_pos + sub * CHUNK >= PAD
        for hd in range(GLA_HEADS):
            ks = slice(hd * dk, (hd + 1) * dk)
            vs = slice(hd * dv, (hd + 1) * dv)
            q = q_ref[0, rs, ks] * (dk ** -0.5)
            k = jnp.where(key_ok, k_ref[0, rs, ks], 0.0)
            v = v_ref[0, rs, vs].astype(BF16)
            b = bcum_all[rs, ks]
            b_ref = b[CHUNK // 2 - 1:CHUNK // 2, :]
            b_last = b[CHUNK - 1:CHUNK, :]
            e_pos = jnp.exp(b - b_ref)
            e_neg = jnp.exp(b_ref - b)
            s_lo = _dot_nt((q * e_pos).astype(BF16), (k * e_neg).astype(BF16))
            s_up = _dot_nt((q * e_neg).astype(BF16), (k * e_pos).astype(BF16))
            st = st_scr[hd]
            o_inter = _dot_nt((q * jnp.exp(b)).astype(BF16), st.astype(BF16))
            k_state = (k * jnp.exp(b_last - b)).astype(BF16)
            st_scr[hd] = st * jnp.exp(b_last) + _dot_tn(v, k_state)
            scores = jnp.where(lower, s_lo, s_up).astype(BF16)
            o = _dot(scores, v) + o_inter
            if zero is not None and sub == 0 and hd == GLA_HEADS // 2:
                o = o + zero
            o = _rms(o, gon)
            gate = g_ref[0, rs, vs]
            o_ref[0, rs, vs] = (o * (gate * jax.nn.sigmoid(gate))).astype(o_ref.dtype)


def _gla_core(proj, la, g_onorm, batch, length, make_casts):
    kd = la.shape[-1]
    vd = (proj.shape[-1] - 2 * kd) // 2
    dk, dv = kd // GLA_HEADS, vd // GLA_HEADS
    n_c = length // GLA_BLOCK
    shift = n_c - LEAD // GLA_BLOCK
    proj = proj.reshape(batch, length, proj.shape[-1])
    la = la.reshape(batch, length, kd)

    def blk(col):
        return lambda b, c: (b, (c + shift) % n_c, col)

    (o,), cast = _host_call(
        functools.partial(_gla_core_kernel, dk=dk, dv=dv),
        make_casts(lambda b, c: b * n_c + c, batch * n_c),
        in_specs=[
            pl.BlockSpec((1, GLA_BLOCK, kd), blk(0)),
            pl.BlockSpec((1, GLA_BLOCK, kd), blk(1)),
            pl.BlockSpec((1, GLA_BLOCK, vd), blk(kd * 2 // vd)),
            pl.BlockSpec((1, GLA_BLOCK, vd), blk(kd * 2 // vd + 1)),
            pl.BlockSpec((1, GLA_BLOCK, kd), blk(0)),
            pl.BlockSpec((1, dv), lambda b, c: (0, 0)),
        ],
        out_specs=[pl.BlockSpec((1, GLA_BLOCK, vd), blk(0))],
        out_shape=[jax.ShapeDtypeStruct((batch, length, vd), BF16)],
        args=(proj, proj, proj, proj, la, g_onorm),
        grid=(batch, n_c),
        scratch_shapes=[pltpu.VMEM((GLA_HEADS, dv, dk), F32)],
        compiler_params=_params(("arbitrary", "arbitrary")),
        name="gla_core",
    )
    return o, cast


def _proj_res_kernel(x_ref, w_ref, res_ref, o_ref, *, side):
    zero = side()
    res = res_ref[0] if zero is None else res_ref[0] + zero
    o_ref[0] = res + _dot(x_ref[0], w_ref[...])


def _proj_res(x, w, res, tm, tn, make_casts=lambda lin, n_steps: ()):
    batch, rows, d = x.shape
    n = w.shape[1]
    n_i, n_j = rows // tm, n // tn
    (o,), cast = _host_call(
        _proj_res_kernel,
        make_casts(lambda b, i, j: (b * n_i + i) * n_j + j, batch * n_i * n_j),
        in_specs=[
            pl.BlockSpec((1, tm, d), lambda b, i, j: (b, i, 0)),
            pl.BlockSpec((d, tn), lambda b, i, j: (0, j)),
            pl.BlockSpec((1, tm, tn), lambda b, i, j: (b, i, j)),
        ],
        out_specs=[pl.BlockSpec((1, tm, tn), lambda b, i, j: (b, i, j))],
        out_shape=[jax.ShapeDtypeStruct((batch, rows, n), F32)],
        args=(x, w, res),
        grid=(batch, n_i, n_j),
        compiler_params=_params(("arbitrary", "arbitrary", "arbitrary")),
        name="proj_res",
    )
    return o, cast


def _ffn_kernel(h_ref, g_ref, wg_ref, wu_ref, wd_ref, o_ref, y_scr, *, side):
    @pl.when(pl.program_id(2) == 0)
    def _():
        h = h_ref[0]
        y_scr[...] = _rms(h, g_ref[...]).astype(BF16)
        o_ref[0] = h

    zero = side()
    y = y_scr[...]
    gate = _dot(y, wg_ref[...])
    if zero is not None:
        gate = gate + zero
    up = _dot(y, wu_ref[...])
    act = (gate * jax.nn.sigmoid(gate) * up).astype(BF16)
    o_ref[0] += _dot(act, wd_ref[...])


def _ffn(h, g, w_gate, w_up, w_down, rows, tm, tf, make_casts=lambda lin, n_steps: ()):
    batch, _, d = h.shape
    d_ff = w_gate.shape[-1]
    n_i, n_f = rows // tm, d_ff // tf
    (o,), cast = _host_call(
        _ffn_kernel,
        make_casts(lambda b, i, f: (b * n_i + i) * n_f + f, batch * n_i * n_f),
        in_specs=[
            pl.BlockSpec((1, tm, d), lambda b, i, f: (b, i, 0)),
            pl.BlockSpec((1, d), lambda b, i, f: (0, 0)),
            pl.BlockSpec((d, tf), lambda b, i, f: (0, f)),
            pl.BlockSpec((d, tf), lambda b, i, f: (0, f)),
            pl.BlockSpec((tf, d), lambda b, i, f: (f, 0)),
        ],
        out_specs=[pl.BlockSpec((1, tm, d), lambda b, i, f: (b, i, 0))],
        out_shape=[jax.ShapeDtypeStruct((batch, rows, d), F32)],
        args=(h, g, w_gate, w_up, w_down),
        grid=(batch, n_i, n_f),
        scratch_shapes=[pltpu.VMEM((tm, d), BF16)],
        compiler_params=_params(("arbitrary", "arbitrary", "arbitrary")),
        name="ffn",
    )
    return o, cast


def _kv_proj_kernel(h_ref, g_ref, w_ref, gk_ref, o_ref, y_scr, *, n_key_tiles):
    j = pl.program_id(2)

    @pl.when(j == 0)
    def _():
        y_scr[...] = _rms(h_ref[0], g_ref[...]).astype(BF16)

    r = _dot(y_scr[...], w_ref[...])
    normed = jnp.concatenate(_head_rms(r, gk_ref[...], 1.0), axis=1)
    o_ref[0] = jnp.where(j < n_key_tiles, normed, r).astype(o_ref.dtype)


def _kv_proj(h, g, w_kv, g_k, tm, tn):
    batch, rows, d = h.shape
    n = w_kv.shape[1]
    return pl.pallas_call(
        functools.partial(_kv_proj_kernel, n_key_tiles=(n // 2) // tn),
        grid=(batch, rows // tm, n // tn),
        in_specs=[
            pl.BlockSpec((1, tm, d), lambda b, i, j: (b, i, 0)),
            pl.BlockSpec((1, d), lambda b, i, j: (0, 0)),
            pl.BlockSpec((d, tn), lambda b, i, j: (0, j)),
            pl.BlockSpec((1, SB_HEAD_DIM), lambda b, i, j: (0, 0)),
        ],
        out_specs=pl.BlockSpec((1, tm, tn), lambda b, i, j: (b, i, j)),
        out_shape=jax.ShapeDtypeStruct((batch, rows, n), BF16),
        scratch_shapes=[pltpu.VMEM((tm, d), BF16)],
        compiler_params=_params(("parallel", "parallel", "arbitrary")),
        name="kv_proj",
    )(h, g, w_kv, g_k)


def _q_proj_kernel(h_ref, g_ref, w_ref, gq_ref, o_ref, y_scr):
    @pl.when(pl.program_id(2) == 0)
    def _():
        y_scr[...] = _rms(h_ref[0], g_ref[...]).astype(BF16)

    r = _dot(y_scr[...], w_ref[...])
    heads = _head_rms(r, gq_ref[...], SB_HEAD_DIM ** -0.5)
    o_ref[0] = jnp.concatenate(heads, axis=1).astype(o_ref.dtype)


def _q_proj(h, g, w_q, g_q, rows, tm, tn):
    batch, _, d = h.shape
    n = w_q.shape[1]
    return pl.pallas_call(
        _q_proj_kernel,
        grid=(batch, rows // tm, n // tn),
        in_specs=[
            pl.BlockSpec((1, tm, d), lambda b, i, j: (b, i, 0)),
            pl.BlockSpec((1, d), lambda b, i, j: (0, 0)),
            pl.BlockSpec((d, tn), lambda b, i, j: (0, j)),
            pl.BlockSpec((1, SB_HEAD_DIM), lambda b, i, j: (0, 0)),
        ],
        out_specs=pl.BlockSpec((1, tm, tn), lambda b, i, j: (b, i, j)),
        out_shape=jax.ShapeDtypeStruct((batch, rows, n), BF16),
        scratch_shapes=[pltpu.VMEM((tm, d), BF16)],
        compiler_params=_params(("parallel", "parallel", "arbitrary")),
        name="q_proj",
    )(h, g, w_q, g_q)


def _sb_scores(q_ref, k_ref, z_scr, start):
    d = SB_HEAD_DIM
    for hd in range(SB_HG):
        hs = slice(hd * d, (hd + 1) * d)
        z_scr[hd] = _dot_nt(q_ref[0, :, hs], k_ref[0, pl.ds(start, d), hs])


def _sb_sweep(q_ref, k_ref, v_ref, u, acc_scr, car_scr, z_scr, start, next_start, mask):
    d = SB_HEAD_DIM
    cats, logits = [], []
    for hd in range(SB_HG):
        z = z_scr[hd]
        lf = _log_sigmoid(-z)
        if mask is not None:
            lf = jnp.where(mask, lf, 0.0)
        hi, lo = _split_bf16(lf)
        cats.append(jnp.concatenate([hi, lo], axis=1))
        logits.append(lf + z)
    sums = [_dot(c, u) for c in cats]
    if next_start is not None:
        _sb_scores(q_ref, k_ref, z_scr, next_start)
    atts = []
    alive = None
    for hd in range(SB_HG):
        carry = car_scr[hd]
        att = jnp.exp(logits[hd] + sums[hd][:, :d] + carry)
        if mask is not None:
            att = jnp.where(mask, att, 0.0)
        atts.append(att.astype(BF16))
        carry = carry + sums[hd][:, d:]
        car_scr[hd] = carry
        alive = carry if alive is None else jnp.maximum(alive, carry)
    for hd in range(SB_HG):
        acc_scr[hd] += _dot(atts[hd], v_ref[0, pl.ds(start, d), hd * d:(hd + 1) * d])
    return jnp.max(alive)


def _sb_kernel(q_ref, k_ref, v_ref, u_ref, o_ref, acc_scr, car_scr, z_scr, *, lead_start):
    i = pl.program_id(2)
    d = SB_HEAD_DIM
    acc_scr[...] = jnp.zeros_like(acc_scr)
    car_scr[...] = jnp.zeros_like(car_scr)
    row = lax.broadcasted_iota(jnp.int32, (d, d), 0)
    col = lax.broadcasted_iota(jnp.int32, (d, d), 1)
    u = u_ref[...]
    sweep = functools.partial(_sb_sweep, q_ref, k_ref, v_ref, u, acc_scr, car_scr, z_scr)

    def block_start(jb):
        return pl.multiple_of(jnp.where(jb >= 0, jb * d, lead_start), d)

    _sb_scores(q_ref, k_ref, z_scr, block_start(i))
    alive0 = sweep(block_start(i), block_start(i - 1), col < row)

    def cond(state):
        jb, alive = state
        return jnp.logical_and(jb >= 0, alive > SB_DEAD_LOG)

    def body(state):
        jb, _ = state
        return jb - 1, sweep(block_start(jb), block_start(jb - 1), None)

    jb, alive = lax.while_loop(cond, body, (i - 1, alive0))

    @pl.when(jnp.logical_and(jb < 0, alive > SB_DEAD_LOG))
    def _():
        sweep(lead_start, None, col >= PAD)

    for hd in range(SB_HG):
        o_ref[0, :, hd * d:(hd + 1) * d] = acc_scr[hd].astype(o_ref.dtype)


def _sb_attn(q, kv, u):
    batch, rows, dm = q.shape
    length = kv.shape[1]
    gw = SB_HG * SB_HEAD_DIM
    return pl.pallas_call(
        functools.partial(_sb_kernel, lead_start=rows),
        grid=(batch, dm // gw, rows // SB_TQ),
        in_specs=[
            pl.BlockSpec((1, SB_TQ, gw), lambda b, h, i: (b, i, h)),
            pl.BlockSpec((1, length, gw), lambda b, h, i: (b, 0, h)),
            pl.BlockSpec((1, length, gw), lambda b, h, i: (b, 0, h + dm // gw)),
            pl.BlockSpec((2 * SB_HEAD_DIM, 2 * SB_HEAD_DIM), lambda b, h, i: (0, 0)),
        ],
        out_specs=pl.BlockSpec((1, SB_TQ, gw), lambda b, h, i: (b, i, h)),
        out_shape=jax.ShapeDtypeStruct((batch, rows, dm), BF16),
        scratch_shapes=[pltpu.VMEM((SB_HG, SB_TQ, SB_HEAD_DIM), F32)] * 3,
        compiler_params=_params(("parallel", "parallel", "arbitrary")),
        name="sb_attn",
    )(q, kv, kv, u)


def _suffix_matrix():
    r = jnp.arange(SB_HEAD_DIM)
    later = (r[:, None] > r[None, :]).astype(BF16)
    half = jnp.concatenate([later, jnp.ones((SB_HEAD_DIM, SB_HEAD_DIM), BF16)], axis=1)
    return jnp.concatenate([half, half], axis=0)


def kernel(x, meta_tokens, g_norm_a, w_in_a, w_gate_up_a, b_gate_a, g_onorm_a, w_out_a, g_kv_norm, w_kv, g_k, g_norm_b, w_q_b, g_q_b, w_o_b, g_ffn_norm, w_ffn_gate, w_ffn_up, w_ffn_down):
    batch, seq, d = x.shape
    length = seq + LEAD
    n_a = g_norm_a.shape[0]
    n_b = g_norm_b.shape[0]
    n_main = w_in_a.shape[-1] - GATE_RANK

    lead = jnp.concatenate([jnp.zeros((PAD, d), x.dtype), meta_tokens.astype(x.dtype)], axis=0)
    h = jnp.concatenate([x, jnp.broadcast_to(lead[None], (batch, LEAD, d))], axis=1)

    def row(v):
        return v.reshape(1, -1).astype(F32)

    ready = {}

    def bf(name, w, layer=None):
        if (name, layer) in ready:
            return ready.pop((name, layer))
        return (w if layer is None else w[layer]).astype(BF16)

    def plan(jobs):
        def make_casts(lin, n_steps):
            return [_cast_job(w, layer, block, lin, n_steps) for _, w, layer, block in jobs]

        def collect(converted):
            for (name, _, layer, _), c in zip(jobs, converted):
                ready[(name, layer)] = c
        return make_casts, collect

    def ffn_jobs(layer, n_in, n_down):
        d_ff = w_ffn_gate.shape[-1]
        return [("gate", w_ffn_gate, layer, (d // n_in, d_ff)),
                ("up", w_ffn_up, layer, (d // n_in, d_ff)),
                ("down", w_ffn_down, layer, (d_ff // n_down, d))]

    def ffn_weights(layer):
        return (bf("gate", w_ffn_gate, layer), bf("up", w_ffn_up, layer),
                bf("down", w_ffn_down, layer))

    w_in_bf = w_in_a.astype(BF16)
    for layer in range(n_a):
        last_a = layer == n_a - 1
        w_a = jnp.pad(w_in_bf[layer][:, n_main:], ((0, 0), (0, LANES - GATE_RANK)))
        w_gu = jnp.pad(w_gate_up_a[layer], ((0, LANES - GATE_RANK), (0, 0))).astype(BF16)
        proj, la = _gla_proj(h.reshape(batch * length, d), row(g_norm_a[layer]), w_in_bf, layer,
                             w_a, w_gu, row(b_gate_a[layer]), tm=1056, tn=1024)
        make_casts, collect = plan([("out", w_out_a, layer, (128, 1024))])
        o, converted = _gla_core(proj, la, row(g_onorm_a[layer]), batch, length, make_casts)
        collect(converted)
        jobs = ffn_jobs(layer, 32, 32)
        if last_a:
            jobs.append(("kv", w_kv, None, (128, w_kv.shape[1] // 2)))
        make_casts, collect = plan(jobs)
        h, converted = _proj_res(o, bf("out", w_out_a, layer), h, tm=528, tn=1024,
                                 make_casts=make_casts)
        collect(converted)
        jobs = []
        if last_a and n_b:
            jobs = ([("q", w_q_b, 0, (128, 1024)), ("o", w_o_b, 0, (128, 1024))]
                    + ffn_jobs(n_a, 64, 88))
        make_casts, collect = plan(jobs)
        h, converted = _ffn(h, row(g_ffn_norm[layer]), *ffn_weights(layer),
                            rows=length, tm=704, tf=512, make_casts=make_casts)
        collect(converted)

    kv = _kv_proj(h, row(g_kv_norm), bf("kv", w_kv), row(g_k), tm=1056, tn=1024)
    u = _suffix_matrix()
    for j in range(n_b):
        layer = n_a + j
        q = _q_proj(h, row(g_norm_b[j]), bf("q", w_q_b, j), row(g_q_b[j]),
                    rows=seq, tm=1024, tn=1024)
        o = _sb_attn(q, kv, u)
        h, _ = _proj_res(o, bf("o", w_o_b, j), h, tm=1024, tn=1024)
        jobs = []
        if j + 1 < n_b:
            jobs = ([("q", w_q_b, j + 1, (128, 1024)), ("o", w_o_b, j + 1, (128, 1024))]
                    + ffn_jobs(layer + 1, 64, 88))
        make_casts, collect = plan(jobs)
        h, converted = _ffn(h, row(g_ffn_norm[layer]), *ffn_weights(layer),
                            rows=seq, tm=512, tf=512, make_casts=make_casts)
        collect(converted)
    return h
```

```python
import functools
from typing import NamedTuple

import jax
import jax.numpy as jnp
from jax import lax
from jax.experimental import pallas as pl
from jax.experimental.pallas import tpu as pltpu

F32 = jnp.float32
BF16 = jnp.bfloat16

CHUNK = 64
N_META = 16
LEAD = 128
PAD = LEAD - N_META
GLA_HEADS = 4
GLA_BLOCK = 2 * CHUNK
GATE_RANK = 16
GATE_TAU = 16.0
SB_HEAD_DIM = 128
EPS = 1e-6

LANES = 128
VMEM_LIMIT = 56 * 1024 * 1024

SB_DEAD_COST = 88.0
LOG2E = 1.4426950408889634
SB_TQ = SB_HEAD_DIM
SB_HG = 8


def _params(sem):
    return pltpu.CompilerParams(dimension_semantics=sem, vmem_limit_bytes=VMEM_LIMIT)


def _dot(a, b):
    return jnp.dot(a, b, preferred_element_type=F32)


def _dot_nt(a, b):
    return lax.dot_general(a, b, (((1,), (1,)), ((), ())), preferred_element_type=F32)


def _dot_tn(a, b):
    return lax.dot_general(a, b, (((0,), (0,)), ((), ())), preferred_element_type=F32)


def _rms(x, g):
    ms = jnp.mean(x * x, axis=-1, keepdims=True)
    return x * lax.rsqrt(ms + EPS) * g


def _log_sigmoid(x):
    return jnp.minimum(x, 0.0) - jnp.log(1.0 + jnp.exp(-jnp.abs(x)))


def _split_bf16(x):
    hi = x.astype(BF16)
    lo = (x - hi.astype(F32)).astype(BF16)
    return hi, lo


_PROBE_ROWS = 16


class _Cast(NamedTuple):
    src: jax.Array
    in_spec: pl.BlockSpec
    out_spec: pl.BlockSpec
    out_shape: jax.ShapeDtypeStruct


def _cast_job(w, layer, block, lin, n_steps):
    rows, cols = w.shape[-2:]
    br, bc = block
    n_c = cols // bc
    n_blocks = (rows // br) * n_c
    assert rows % br == 0 and cols % bc == 0 and n_blocks <= n_steps
    assert br % _PROBE_ROWS == 0 and br >= 2 * _PROBE_ROWS

    def idx(*g):
        blk = lin(*g) * n_blocks // n_steps
        return blk // n_c, blk % n_c

    if w.ndim == 3:
        in_spec = pl.BlockSpec((None, br, bc), lambda *g: (layer, *idx(*g)))
    else:
        in_spec = pl.BlockSpec((br, bc), idx)
    return _Cast(w, in_spec, pl.BlockSpec((br, bc), idx), jax.ShapeDtypeStruct((rows, cols), BF16))


def _hosting(body, n_in, n_out, n_cast):
    def kernel(*refs):
        a, b, c = n_in, n_in + n_cast, n_in + n_cast + n_out

        def side():
            if not n_cast:
                return None
            off = pl.multiple_of((pl.program_id(0) % 2) * _PROBE_ROWS, _PROBE_ROWS)
            bits = None
            for src, dst in zip(refs[a:b], refs[c:c + n_cast]):
                dst[...] = src[...].astype(dst.dtype)
                probe = pltpu.bitcast(dst[pl.ds(off, _PROBE_ROWS), :], jnp.uint32)
                for l in range(0, probe.shape[1], LANES):
                    piece = probe[:, l:l + LANES]
                    bits = piece if bits is None else bits | piece
            zero = pltpu.bitcast((bits >> 16) >> 16, F32)
            return jnp.max(zero, axis=(0, 1), keepdims=True)

        body(*refs[:a], *refs[b:c], *refs[c + n_cast:], side=side)
    return kernel


def _host_call(body, casts, in_specs, out_specs, out_shape, args, **kwargs):
    n_out = len(out_specs)
    outs = pl.pallas_call(
        _hosting(body, len(in_specs), n_out, len(casts)),
        in_specs=list(in_specs) + [c.in_spec for c in casts],
        out_specs=list(out_specs) + [c.out_spec for c in casts],
        out_shape=list(out_shape) + [c.out_shape for c in casts],
        **kwargs,
    )(*args, *[c.src for c in casts])
    return outs[:n_out], outs[n_out:]


def _head_rms(x, g, scale):
    outs = []
    for h in range(x.shape[1] // LANES):
        xs = x[:, h * LANES:(h + 1) * LANES]
        ms = jnp.mean(xs * xs, axis=-1, keepdims=True)
        outs.append(xs * lax.rsqrt(ms + EPS) * g * scale)
    return outs


def _gla_proj_kernel(h_ref, g_ref, w_ref, wa_ref, proj_ref, a_ref, y_scr, *, side):
    @pl.when(pl.program_id(1) == 0)
    def _():
        y = _rms(h_ref[...], g_ref[...]).astype(BF16)
        y_scr[...] = y
        a_ref[...] = _dot(y, wa_ref[...])

    zero = side()
    r = _dot(y_scr[...], w_ref[...])
    proj_ref[...] = r if zero is None else r + zero


def _gla_proj(h, g, w_in, layer, w_a, tm, tn, make_casts):
    m, d = h.shape
    n = w_in.shape[-1] - GATE_RANK
    n_i, n_j = m // tm, n // tn
    return _host_call(
        _gla_proj_kernel,
        make_casts(lambda i, j: i * n_j + j, n_i * n_j),
        in_specs=[
            pl.BlockSpec((tm, d), lambda i, j: (i, 0)),
            pl.BlockSpec((1, d), lambda i, j: (0, 0)),
            pl.BlockSpec((None, d, tn), lambda i, j: (layer, 0, j)),
            pl.BlockSpec((d, LANES), lambda i, j: (0, 0)),
        ],
        out_specs=[
            pl.BlockSpec((tm, tn), lambda i, j: (i, j)),
            pl.BlockSpec((tm, LANES), lambda i, j: (i, 0)),
        ],
        out_shape=[
            jax.ShapeDtypeStruct((m, n), F32),
            jax.ShapeDtypeStruct((m, LANES), F32),
        ],
        args=(h, g, w_in, w_a),
        grid=(n_i, n_j),
        scratch_shapes=[pltpu.VMEM((tm, d), BF16)],
        compiler_params=_params(("arbitrary", "arbitrary")),
        name="gla_proj",
    )


def _gla_core_kernel(q_ref, k_ref, v_ref, g_ref, a_ref, wgu_ref, bg_ref, gon_ref, o_ref, st_scr, *,
                     dk, dv, side):
    blk = pl.program_id(1)
    rows = q_ref.shape[1]

    @pl.when(blk == 0)
    def _():
        st_scr[...] = jnp.zeros_like(st_scr)

    zero = side()

    row = lax.broadcasted_iota(jnp.int32, (rows, rows), 0)
    col = lax.broadcasted_iota(jnp.int32, (rows, rows), 1)
    same_chunk = (row ^ col) < CHUNK
    tri = jnp.logical_and(col <= row, same_chunk).astype(BF16)
    lower = (lax.broadcasted_iota(jnp.int32, (CHUNK, CHUNK), 0)
             >= lax.broadcasted_iota(jnp.int32, (CHUNK, CHUNK), 1))
    chunk_pos = blk * rows + lax.broadcasted_iota(jnp.int32, (CHUNK, 1), 0)

    la = _log_sigmoid(_dot(a_ref[0].astype(BF16), wgu_ref[...]) + bg_ref[...]) * (1.0 / GATE_TAU)
    la_hi, la_lo = _split_bf16(la)
    bcum_all = _dot(tri, la_hi) + _dot(tri, la_lo)
    gon = gon_ref[...]

    for sub in range(rows // CHUNK):
        rs = slice(sub * CHUNK, (sub + 1) * CHUNK)
        key_ok = chunk_pos + sub * CHUNK >= PAD
        for hd in range(GLA_HEADS):
            ks = slice(hd * dk, (hd + 1) * dk)
            vs = slice(hd * dv, (hd + 1) * dv)
            q = q_ref[0, rs, ks] * (dk ** -0.5)
            k = jnp.where(key_ok, k_ref[0, rs, ks], 0.0)
            v = v_ref[0, rs, vs].astype(BF16)
            b = bcum_all[rs, ks]
            b_ref = b[CHUNK // 2 - 1:CHUNK // 2, :]
            b_last = b[CHUNK - 1:CHUNK, :]
            e_pos = jnp.exp(b - b_ref)
            e_neg = jnp.exp(b_ref - b)
            s_lo = _dot_nt((q * e_pos).astype(BF16), (k * e_neg).astype(BF16))
            s_up = _dot_nt((q * e_neg).astype(BF16), (k * e_pos).astype(BF16))
            st = st_scr[hd]
            o_inter = _dot_nt((q * jnp.exp(b)).astype(BF16), st.astype(BF16))
            k_state = (k * jnp.exp(b_last - b)).astype(BF16)
            st_scr[hd] = st * jnp.exp(b_last) + _dot_tn(v, k_state)
            scores = jnp.where(lower, s_lo, s_up).astype(BF16)
            o = _dot(scores, v) + o_inter
            if zero is not None and sub == 0 and hd == GLA_HEADS // 2:
                o = o + zero
            o = _rms(o, gon)
            gate = g_ref[0, rs, vs]
            o_ref[0, rs, vs] = (o * (gate * jax.nn.sigmoid(gate))).astype(o_ref.dtype)


def _gla_core(proj, a, w_gu, b_gate, g_onorm, batch, length, make_casts):
    kd = w_gu.shape[-1]
    vd = (proj.shape[-1] - 2 * kd) // 2
    dk, dv = kd // GLA_HEADS, vd // GLA_HEADS
    n_c = length // GLA_BLOCK
    shift = n_c - LEAD // GLA_BLOCK
    proj = proj.reshape(batch, length, proj.shape[-1])
    a = a.reshape(batch, length, LANES)

    def blk(col):
        return lambda b, c: (b, (c + shift) % n_c, col)

    (o,), cast = _host_call(
        functools.partial(_gla_core_kernel, dk=dk, dv=dv),
        make_casts(lambda b, c: b * n_c + c, batch * n_c),
        in_specs=[
            pl.BlockSpec((1, GLA_BLOCK, kd), blk(0)),
            pl.BlockSpec((1, GLA_BLOCK, kd), blk(1)),
            pl.BlockSpec((1, GLA_BLOCK, vd), blk(kd * 2 // vd)),
            pl.BlockSpec((1, GLA_BLOCK, vd), blk(kd * 2 // vd + 1)),
            pl.BlockSpec((1, GLA_BLOCK, LANES), blk(0)),
            pl.BlockSpec((LANES, kd), lambda b, c: (0, 0)),
            pl.BlockSpec((1, kd), lambda b, c: (0, 0)),
            pl.BlockSpec((1, dv), lambda b, c: (0, 0)),
        ],
        out_specs=[pl.BlockSpec((1, GLA_BLOCK, vd), blk(0))],
        out_shape=[jax.ShapeDtypeStruct((batch, length, vd), BF16)],
        args=(proj, proj, proj, proj, a, w_gu, b_gate, g_onorm),
        grid=(batch, n_c),
        scratch_shapes=[pltpu.VMEM((GLA_HEADS, dv, dk), F32)],
        compiler_params=_params(("arbitrary", "arbitrary")),
        name="gla_core",
    )
    return o, cast


def _proj_res_kernel(x_ref, w_ref, res_ref, o_ref, *, side):
    zero = side()
    res = res_ref[0] if zero is None else res_ref[0] + zero
    o_ref[0] = res + _dot(x_ref[0], w_ref[...])


def _proj_res(x, w, res, tm, tn, make_casts=lambda lin, n_steps: ()):
    batch, rows, d = x.shape
    n = w.shape[1]
    n_i, n_j = rows // tm, n // tn
    (o,), cast = _host_call(
        _proj_res_kernel,
        make_casts(lambda b, i, j: (b * n_i + i) * n_j + j, batch * n_i * n_j),
        in_specs=[
            pl.BlockSpec((1, tm, d), lambda b, i, j: (b, i, 0)),
            pl.BlockSpec((d, tn), lambda b, i, j: (0, j)),
            pl.BlockSpec((1, tm, tn), lambda b, i, j: (b, i, j)),
        ],
        out_specs=[pl.BlockSpec((1, tm, tn), lambda b, i, j: (b, i, j))],
        out_shape=[jax.ShapeDtypeStruct((batch, rows, n), F32)],
        args=(x, w, res),
        grid=(batch, n_i, n_j),
        compiler_params=_params(("arbitrary", "arbitrary", "arbitrary")),
        name="proj_res",
    )
    return o, cast


def _ffn_kernel(h_ref, g_ref, wg_ref, wu_ref, wd_ref, o_ref, y_scr, *, side):
    @pl.when(pl.program_id(2) == 0)
    def _():
        h = h_ref[0]
        y_scr[...] = _rms(h, g_ref[...]).astype(BF16)
        o_ref[0] = h

    zero = side()
    y = y_scr[...]
    gate = _dot(y, wg_ref[...])
    if zero is not None:
        gate = gate + zero
    up = _dot(y, wu_ref[...])
    act = (gate * jax.nn.sigmoid(gate) * up).astype(BF16)
    o_ref[0] += _dot(act, wd_ref[...])


def _ffn(h, g, w_gate, w_up, w_down, rows, tm, tf, make_casts=lambda lin, n_steps: ()):
    batch, _, d = h.shape
    d_ff = w_gate.shape[-1]
    n_i, n_f = rows // tm, d_ff // tf
    (o,), cast = _host_call(
        _ffn_kernel,
        make_casts(lambda b, i, f: (b * n_i + i) * n_f + f, batch * n_i * n_f),
        in_specs=[
            pl.BlockSpec((1, tm, d), lambda b, i, f: (b, i, 0)),
            pl.BlockSpec((1, d), lambda b, i, f: (0, 0)),
            pl.BlockSpec((d, tf), lambda b, i, f: (0, f)),
            pl.BlockSpec((d, tf), lambda b, i, f: (0, f)),
            pl.BlockSpec((tf, d), lambda b, i, f: (f, 0)),
        ],
        out_specs=[pl.BlockSpec((1, tm, d), lambda b, i, f: (b, i, 0))],
        out_shape=[jax.ShapeDtypeStruct((batch, rows, d), F32)],
        args=(h, g, w_gate, w_up, w_down),
        grid=(batch, n_i, n_f),
        scratch_shapes=[pltpu.VMEM((tm, d), BF16)],
        compiler_params=_params(("arbitrary", "arbitrary", "arbitrary")),
        name="ffn",
    )
    return o, cast


def _kv_proj_kernel(h_ref, g_ref, w_ref, gk_ref, o_ref, y_scr, *, n_key_tiles):
    j = pl.program_id(2)

    @pl.when(j == 0)
    def _():
        y_scr[...] = _rms(h_ref[0], g_ref[...]).astype(BF16)

    r = _dot(y_scr[...], w_ref[...])
    normed = jnp.concatenate(_head_rms(r, gk_ref[...], 1.0), axis=1)
    o_ref[0] = jnp.where(j < n_key_tiles, normed, r).astype(o_ref.dtype)


def _kv_proj(h, g, w_kv, g_k, tm, tn):
    batch, rows, d = h.shape
    n = w_kv.shape[1]
    return pl.pallas_call(
        functools.partial(_kv_proj_kernel, n_key_tiles=(n // 2) // tn),
        grid=(batch, rows // tm, n // tn),
        in_specs=[
            pl.BlockSpec((1, tm, d), lambda b, i, j: (b, i, 0)),
            pl.BlockSpec((1, d), lambda b, i, j: (0, 0)),
            pl.BlockSpec((d, tn), lambda b, i, j: (0, j)),
            pl.BlockSpec((1, SB_HEAD_DIM), lambda b, i, j: (0, 0)),
        ],
        out_specs=pl.BlockSpec((1, tm, tn), lambda b, i, j: (b, i, j)),
        out_shape=jax.ShapeDtypeStruct((batch, rows, n), BF16),
        scratch_shapes=[pltpu.VMEM((tm, d), BF16)],
        compiler_params=_params(("parallel", "parallel", "arbitrary")),
        name="kv_proj",
    )(h, g, w_kv, g_k)


def _q_proj_kernel(h_ref, g_ref, w_ref, gq_ref, o_ref, y_scr):
    @pl.when(pl.program_id(2) == 0)
    def _():
        y_scr[...] = _rms(h_ref[0], g_ref[...]).astype(BF16)

    r = _dot(y_scr[...], w_ref[...])
    heads = _head_rms(r, gq_ref[...], SB_HEAD_DIM ** -0.5)
    o_ref[0] = jnp.concatenate(heads, axis=1).astype(o_ref.dtype)


def _q_proj(h, g, w_q, g_q, rows, tm, tn):
    batch, _, d = h.shape
    n = w_q.shape[1]
    return pl.pallas_call(
        _q_proj_kernel,
        grid=(batch, rows // tm, n // tn),
        in_specs=[
            pl.BlockSpec((1, tm, d), lambda b, i, j: (b, i, 0)),
            pl.BlockSpec((1, d), lambda b, i, j: (0, 0)),
            pl.BlockSpec((d, tn), lambda b, i, j: (0, j)),
            pl.BlockSpec((1, SB_HEAD_DIM), lambda b, i, j: (0, 0)),
        ],
        out_specs=pl.BlockSpec((1, tm, tn), lambda b, i, j: (b, i, j)),
        out_shape=jax.ShapeDtypeStruct((batch, rows, n), BF16),
        scratch_shapes=[pltpu.VMEM((tm, d), BF16)],
        compiler_params=_params(("parallel", "parallel", "arbitrary")),
        name="q_proj",
    )(h, g, w_q, g_q)


def _sb_scores(q_ref, k_ref, z_scr, start):
    d = SB_HEAD_DIM
    for hd in range(SB_HG):
        hs = slice(hd * d, (hd + 1) * d)
        z_scr[hd] = _dot_nt(q_ref[0, :, hs], k_ref[0, pl.ds(start, d), hs])


def _sb_sweep(q_ref, k_ref, v_ref, u, acc_scr, car_scr, z_scr, start, next_start, mask):
    d = SB_HEAD_DIM
    cats, logits = [], []
    for hd in range(SB_HG):
        z = z_scr[hd]
        cost = jnp.maximum(z, 0.0) + jnp.log(1.0 + jnp.exp2(jnp.abs(z) * -LOG2E))
        if mask is not None:
            cost = jnp.where(mask, cost, 0.0)
        hi, lo = _split_bf16(cost)
        cats.append(jnp.concatenate([hi, lo], axis=1))
        logits.append(z - cost)
    sums = [_dot(c, u) for c in cats]
    if next_start is not None:
        _sb_scores(q_ref, k_ref, z_scr, next_start)
    atts = []
    spent = None
    for hd in range(SB_HG):
        carry = car_scr[hd]
        att = jnp.exp(logits[hd] - sums[hd][:, :d] - carry)
        if mask is not None:
            att = jnp.where(mask, att, 0.0)
        atts.append(att.astype(BF16))
        carry = carry + sums[hd][:, d:]
        car_scr[hd] = carry
        spent = carry if spent is None else jnp.minimum(spent, carry)
    for hd in range(SB_HG):
        acc_scr[hd] += _dot(atts[hd], v_ref[0, pl.ds(start, d), hd * d:(hd + 1) * d])
    return jnp.min(spent)


def _sb_kernel(q_ref, k_ref, v_ref, u_ref, o_ref, acc_scr, car_scr, z_scr, *, lead_start):
    i = pl.program_id(2)
    d = SB_HEAD_DIM
    acc_scr[...] = jnp.zeros_like(acc_scr)
    car_scr[...] = jnp.zeros_like(car_scr)
    row = lax.broadcasted_iota(jnp.int32, (d, d), 0)
    col = lax.broadcasted_iota(jnp.int32, (d, d), 1)
    u = u_ref[...]
    sweep = functools.partial(_sb_sweep, q_ref, k_ref, v_ref, u, acc_scr, car_scr, z_scr)

    def block_start(jb):
        return pl.multiple_of(jnp.where(jb >= 0, jb * d, lead_start), d)

    _sb_scores(q_ref, k_ref, z_scr, block_start(i))
    spent0 = sweep(block_start(i), block_start(i - 1), col < row)

    def cond(state):
        jb, spent = state
        return jnp.logical_and(jb >= 0, spent < SB_DEAD_COST)

    def body(state):
        jb, _ = state
        return jb - 1, sweep(block_start(jb), block_start(jb - 1), None)

    jb, spent = lax.while_loop(cond, body, (i - 1, spent0))

    @pl.when(jnp.logical_and(jb < 0, spent < SB_DEAD_COST))
    def _():
        sweep(lead_start, None, col >= PAD)

    for hd in range(SB_HG):
        o_ref[0, :, hd * d:(hd + 1) * d] = acc_scr[hd].astype(o_ref.dtype)


def _sb_attn(q, kv, u):
    batch, rows, dm = q.shape
    length = kv.shape[1]
    gw = SB_HG * SB_HEAD_DIM
    return pl.pallas_call(
        functools.partial(_sb_kernel, lead_start=rows),
        grid=(batch, dm // gw, rows // SB_TQ),
        in_specs=[
            pl.BlockSpec((1, SB_TQ, gw), lambda b, h, i: (b, i, h)),
            pl.BlockSpec((1, length, gw), lambda b, h, i: (b, 0, h)),
            pl.BlockSpec((1, length, gw), lambda b, h, i: (b, 0, h + dm // gw)),
            pl.BlockSpec((2 * SB_HEAD_DIM, 2 * SB_HEAD_DIM), lambda b, h, i: (0, 0)),
        ],
        out_specs=pl.BlockSpec((1, SB_TQ, gw), lambda b, h, i: (b, i, h)),
        out_shape=jax.ShapeDtypeStruct((batch, rows, dm), BF16),
        scratch_shapes=[pltpu.VMEM((SB_HG, SB_TQ, SB_HEAD_DIM), F32)] * 3,
        compiler_params=_params(("parallel", "parallel", "arbitrary")),
        name="sb_attn",
    )(q, kv, kv, u)


def _suffix_matrix():
    r = jnp.arange(SB_HEAD_DIM)
    later = (r[:, None] > r[None, :]).astype(BF16)
    half = jnp.concatenate([later, jnp.ones((SB_HEAD_DIM, SB_HEAD_DIM), BF16)], axis=1)
    return jnp.concatenate([half, half], axis=0)


def kernel(x, meta_tokens, g_norm_a, w_in_a, w_gate_up_a, b_gate_a, g_onorm_a, w_out_a, g_kv_norm, w_kv, g_k, g_norm_b, w_q_b, g_q_b, w_o_b, g_ffn_norm, w_ffn_gate, w_ffn_up, w_ffn_down):
    batch, seq, d = x.shape
    length = seq + LEAD
    n_a = g_norm_a.shape[0]
    n_b = g_norm_b.shape[0]
    n_main = w_in_a.shape[-1] - GATE_RANK

    lead = jnp.concatenate([jnp.zeros((PAD, d), x.dtype), meta_tokens.astype(x.dtype)], axis=0)
    h = jnp.concatenate([x, jnp.broadcast_to(lead[None], (batch, LEAD, d))], axis=1)

    def row(v):
        return v.reshape(1, -1).astype(F32)

    ready = {}

    def bf(name, w, layer=None):
        if (name, layer) in ready:
            return ready.pop((name, layer))
        return (w if layer is None else w[layer]).astype(BF16)

    def plan(jobs):
        def make_casts(lin, n_steps):
            return [_cast_job(w, layer, block, lin, n_steps) for _, w, layer, block in jobs]

        def collect(converted):
            for (name, _, layer, _), c in zip(jobs, converted):
                ready[(name, layer)] = c
        return make_casts, collect

    def ffn_jobs(layer, in_block, down_block):
        return [("gate", w_ffn_gate, layer, in_block), ("up", w_ffn_up, layer, in_block),
                ("down", w_ffn_down, layer, down_block)]

    d_ff = w_ffn_gate.shape[-1]

    def ffn_weights(layer):
        return (bf("gate", w_ffn_gate, layer), bf("up", w_ffn_up, layer),
                bf("down", w_ffn_down, layer))

    w_in_bf = w_in_a.astype(BF16)
    for layer in range(n_a):
        last_a = layer == n_a - 1
        w_a = jnp.pad(w_in_bf[layer][:, n_main:], ((0, 0), (0, LANES - GATE_RANK)))
        w_gu = jnp.pad(w_gate_up_a[layer], ((0, LANES - GATE_RANK), (0, 0))).astype(BF16)
        make_casts, collect = plan(ffn_jobs(layer, (d, LANES), (LANES, d)))
        (proj, a), converted = _gla_proj(h.reshape(batch * length, d), row(g_norm_a[layer]),
                                         w_in_bf, layer, w_a, tm=1056, tn=1024,
                                         make_casts=make_casts)
        collect(converted)
        make_casts, collect = plan([("out", w_out_a, layer, (128, 1024))])
        o, converted = _gla_core(proj, a, w_gu, row(b_gate_a[layer]), row(g_onorm_a[layer]),
                                 batch, length, make_casts)
        collect(converted)
        make_casts, collect = plan([("kv", w_kv, None, (128, w_kv.shape[1]))] if last_a else [])
        h, converted = _proj_res(o, bf("out", w_out_a, layer), h, tm=1056, tn=1024,
                                 make_casts=make_casts)
        collect(converted)
        jobs = []
        if last_a and n_b:
            jobs = ([("q", w_q_b, 0, (128, 1024)), ("o", w_o_b, 0, (128, 1024))]
                    + ffn_jobs(n_a, (32, d_ff), (64, d)))
        make_casts, collect = plan(jobs)
        h, converted = _ffn(h, row(g_ffn_norm[layer]), *ffn_weights(layer),
                            rows=length, tm=704, tf=512, make_casts=make_casts)
        collect(converted)

    kv = _kv_proj(h, row(g_kv_norm), bf("kv", w_kv), row(g_k), tm=1056, tn=1024)
    u = _suffix_matrix()
    for j in range(n_b):
        layer = n_a + j
        q = _q_proj(h, row(g_norm_b[j]), bf("q", w_q_b, j), row(g_q_b[j]),
                    rows=seq, tm=1024, tn=1024)
        o = _sb_attn(q, kv, u)
        h, _ = _proj_res(o, bf("o", w_o_b, j), h, tm=1024, tn=1024)
        jobs = []
        if j + 1 < n_b:
            jobs = ([("q", w_q_b, j + 1, (128, 1024)), ("o", w_o_b, j + 1, (128, 1024))]
                    + ffn_jobs(layer + 1, (32, d_ff), (64, d)))
        make_casts, collect = plan(jobs)
        h, converted = _ffn(h, row(g_ffn_norm[layer]), *ffn_weights(layer),
                            rows=seq, tm=512, tf=512, make_casts=make_casts)
        collect(converted)
    return h
```

```python
import functools
from typing import NamedTuple

import jax
import jax.numpy as jnp
from jax import lax
from jax.experimental import pallas as pl
from jax.experimental.pallas import tpu as pltpu

F32 = jnp.float32
BF16 = jnp.bfloat16

CHUNK = 64
N_META = 16
LEAD = 128
PAD = LEAD - N_META
GLA_HEADS = 4
GLA_BLOCK = 2 * CHUNK
GATE_RANK = 16
GATE_TAU = 16.0
SB_HEAD_DIM = 128
EPS = 1e-6

LANES = 128
VMEM_LIMIT = 56 * 1024 * 1024

SB_DEAD_COST = 88.0
LOG2E = 1.4426950408889634
SB_TQ = 2 * SB_HEAD_DIM
SB_HG = 8


def _params(sem):
    return pltpu.CompilerParams(dimension_semantics=sem, vmem_limit_bytes=VMEM_LIMIT)


def _dot(a, b):
    return jnp.dot(a, b, preferred_element_type=F32)


def _dot_nt(a, b):
    return lax.dot_general(a, b, (((1,), (1,)), ((), ())), preferred_element_type=F32)


def _dot_tn(a, b):
    return lax.dot_general(a, b, (((0,), (0,)), ((), ())), preferred_element_type=F32)


def _rms(x, g):
    ms = jnp.mean(x * x, axis=-1, keepdims=True)
    return x * lax.rsqrt(ms + EPS) * g


def _log_sigmoid(x):
    return jnp.minimum(x, 0.0) - jnp.log(1.0 + jnp.exp(-jnp.abs(x)))


def _split_bf16(x):
    hi = x.astype(BF16)
    lo = (x - hi.astype(F32)).astype(BF16)
    return hi, lo


_PROBE_ROWS = 16


class _Cast(NamedTuple):
    src: jax.Array
    in_spec: pl.BlockSpec
    out_spec: pl.BlockSpec
    out_shape: jax.ShapeDtypeStruct


def _cast_job(w, layer, block, lin, n_steps):
    rows, cols = w.shape[-2:]
    br, bc = block
    n_c = cols // bc
    n_blocks = (rows // br) * n_c
    assert rows % br == 0 and cols % bc == 0 and n_blocks <= n_steps
    assert br % _PROBE_ROWS == 0 and br >= 2 * _PROBE_ROWS

    def idx(*g):
        blk = lin(*g) * n_blocks // n_steps
        return blk // n_c, blk % n_c

    if w.ndim == 3:
        in_spec = pl.BlockSpec((None, br, bc), lambda *g: (layer, *idx(*g)))
    else:
        in_spec = pl.BlockSpec((br, bc), idx)
    return _Cast(w, in_spec, pl.BlockSpec((br, bc), idx), jax.ShapeDtypeStruct((rows, cols), BF16))


def _hosting(body, n_in, n_out, n_cast):
    def kernel(*refs):
        a, b, c = n_in, n_in + n_cast, n_in + n_cast + n_out

        def side():
            if not n_cast:
                return None
            off = pl.multiple_of((pl.program_id(0) % 2) * _PROBE_ROWS, _PROBE_ROWS)
            bits = None
            for src, dst in zip(refs[a:b], refs[c:c + n_cast]):
                dst[...] = src[...].astype(dst.dtype)
                probe = pltpu.bitcast(dst[pl.ds(off, _PROBE_ROWS), :], jnp.uint32)
                for l in range(0, probe.shape[1], LANES):
                    piece = probe[:, l:l + LANES]
                    bits = piece if bits is None else bits | piece
            zero = pltpu.bitcast((bits >> 16) >> 16, F32)
            return jnp.max(zero, axis=(0, 1), keepdims=True)

        body(*refs[:a], *refs[b:c], *refs[c + n_cast:], side=side)
    return kernel


def _host_call(body, casts, in_specs, out_specs, out_shape, args, **kwargs):
    n_out = len(out_specs)
    outs = pl.pallas_call(
        _hosting(body, len(in_specs), n_out, len(casts)),
        in_specs=list(in_specs) + [c.in_spec for c in casts],
        out_specs=list(out_specs) + [c.out_spec for c in casts],
        out_shape=list(out_shape) + [c.out_shape for c in casts],
        **kwargs,
    )(*args, *[c.src for c in casts])
    return outs[:n_out], outs[n_out:]


def _head_rms(x, g, scale):
    outs = []
    for h in range(x.shape[1] // LANES):
        xs = x[:, h * LANES:(h + 1) * LANES]
        ms = jnp.mean(xs * xs, axis=-1, keepdims=True)
        outs.append(xs * lax.rsqrt(ms + EPS) * g * scale)
    return outs


def _gla_proj_kernel(h_ref, g_ref, w_ref, wa_ref, proj_ref, a_ref, y_scr, *, side):
    @pl.when(pl.program_id(2) == 0)
    def _():
        y = _rms(h_ref[0], g_ref[...]).astype(BF16)
        y_scr[...] = y
        a_ref[0] = _dot(y, wa_ref[...])

    zero = side()
    r = _dot(y_scr[...], w_ref[...])
    proj_ref[0] = r if zero is None else r + zero


def _gla_proj(h, g, w_in, layer, w_a, rows, length, tm, tn, make_casts):
    batch, _, d = h.shape
    n = w_in.shape[-1] - GATE_RANK
    n_i, n_j = rows // tm, n // tn
    return _host_call(
        _gla_proj_kernel,
        make_casts(lambda b, i, j: (b * n_i + i) * n_j + j, batch * n_i * n_j),
        in_specs=[
            pl.BlockSpec((1, tm, d), lambda b, i, j: (b, i, 0)),
            pl.BlockSpec((1, d), lambda b, i, j: (0, 0)),
            pl.BlockSpec((None, d, tn), lambda b, i, j: (layer, 0, j)),
            pl.BlockSpec((d, LANES), lambda b, i, j: (0, 0)),
        ],
        out_specs=[
            pl.BlockSpec((1, tm, tn), lambda b, i, j: (b, i, j)),
            pl.BlockSpec((1, tm, LANES), lambda b, i, j: (b, i, 0)),
        ],
        out_shape=[
            jax.ShapeDtypeStruct((batch, length, n), F32),
            jax.ShapeDtypeStruct((batch, length, LANES), F32),
        ],
        args=(h, g, w_in, w_a),
        grid=(batch, n_i, n_j),
        scratch_shapes=[pltpu.VMEM((tm, d), BF16)],
        compiler_params=_params(("arbitrary", "arbitrary", "arbitrary")),
        name="gla_proj",
    )


def _gla_proj_lead_kernel(lead_ref, g_ref, w_ref, wa_ref, proj_in, a_in, proj_ref, a_ref):
    del proj_in, a_in
    y = _rms(lead_ref[...], g_ref[...]).astype(BF16)
    a = _dot(y, wa_ref[...])
    r = _dot(y, w_ref[...])
    for b in range(proj_ref.shape[0]):
        a_ref[b] = a
        proj_ref[b] = r


def _gla_proj_lead(lead, g, w_in, layer, w_a, proj, a, tn):
    batch, length, n = proj.shape
    d = lead.shape[1]
    blk = length // LEAD - 1
    return pl.pallas_call(
        _gla_proj_lead_kernel,
        grid=(n // tn,),
        in_specs=[
            pl.BlockSpec((LEAD, d), lambda j: (0, 0)),
            pl.BlockSpec((1, d), lambda j: (0, 0)),
            pl.BlockSpec((None, d, tn), lambda j: (layer, 0, j)),
            pl.BlockSpec((d, LANES), lambda j: (0, 0)),
            pl.BlockSpec(memory_space=pl.ANY),
            pl.BlockSpec(memory_space=pl.ANY),
        ],
        out_specs=[
            pl.BlockSpec((batch, LEAD, tn), lambda j: (0, blk, j)),
            pl.BlockSpec((batch, LEAD, LANES), lambda j: (0, blk, 0)),
        ],
        out_shape=[jax.ShapeDtypeStruct(proj.shape, proj.dtype),
                   jax.ShapeDtypeStruct(a.shape, a.dtype)],
        input_output_aliases={4: 0, 5: 1},
        compiler_params=_params(("arbitrary",)),
        name="gla_proj_lead",
    )(lead, g, w_in, w_a, proj, a)


def _gla_core_kernel(q_ref, k_ref, v_ref, g_ref, a_ref, wgu_ref, bg_ref, gon_ref, o_ref, st_scr, *,
                     dk, dv, side):
    blk = pl.program_id(1)
    rows = q_ref.shape[1]

    @pl.when(blk == 0)
    def _():
        st_scr[...] = jnp.zeros_like(st_scr)

    zero = side()

    row = lax.broadcasted_iota(jnp.int32, (rows, rows), 0)
    col = lax.broadcasted_iota(jnp.int32, (rows, rows), 1)
    same_chunk = (row ^ col) < CHUNK
    tri = jnp.logical_and(col <= row, same_chunk).astype(BF16)
    lower = (lax.broadcasted_iota(jnp.int32, (CHUNK, CHUNK), 0)
             >= lax.broadcasted_iota(jnp.int32, (CHUNK, CHUNK), 1))
    chunk_pos = blk * rows + lax.broadcasted_iota(jnp.int32, (CHUNK, 1), 0)

    la = _log_sigmoid(_dot(a_ref[0].astype(BF16), wgu_ref[...]) + bg_ref[...]) * (1.0 / GATE_TAU)
    la_hi, la_lo = _split_bf16(la)
    bcum_all = _dot(tri, la_hi) + _dot(tri, la_lo)
    gon = gon_ref[...]

    for sub in range(rows // CHUNK):
        rs = slice(sub * CHUNK, (sub + 1) * CHUNK)
        key_ok = chunk_pos + sub * CHUNK >= PAD
        for hd in range(GLA_HEADS):
            ks = slice(hd * dk, (hd + 1) * dk)
            vs = slice(hd * dv, (hd + 1) * dv)
            q = q_ref[0, rs, ks] * (dk ** -0.5)
            k = jnp.where(key_ok, k_ref[0, rs, ks], 0.0)
            v = v_ref[0, rs, vs].astype(BF16)
            b = bcum_all[rs, ks]
            b_ref = b[CHUNK // 2 - 1:CHUNK // 2, :]
            b_last = b[CHUNK - 1:CHUNK, :]
            e_pos = jnp.exp(b - b_ref)
            e_neg = jnp.exp(b_ref - b)
            s_lo = _dot_nt((q * e_pos).astype(BF16), (k * e_neg).astype(BF16))
            s_up = _dot_nt((q * e_neg).astype(BF16), (k * e_pos).astype(BF16))
            st = st_scr[hd]
            o_inter = _dot_nt((q * jnp.exp(b)).astype(BF16), st.astype(BF16))
            k_state = (k * jnp.exp(b_last - b)).astype(BF16)
            st_scr[hd] = st * jnp.exp(b_last) + _dot_tn(v, k_state)
            scores = jnp.where(lower, s_lo, s_up).astype(BF16)
            o = _dot(scores, v) + o_inter
            if zero is not None and sub == 0 and hd == GLA_HEADS // 2:
                o = o + zero
            o = _rms(o, gon)
            gate = g_ref[0, rs, vs]
            o_ref[0, rs, vs] = (o * (gate * jax.nn.sigmoid(gate))).astype(o_ref.dtype)


def _gla_core(proj, a, w_gu, b_gate, g_onorm, batch, length, make_casts):
    kd = w_gu.shape[-1]
    vd = (proj.shape[-1] - 2 * kd) // 2
    dk, dv = kd // GLA_HEADS, vd // GLA_HEADS
    n_c = length // GLA_BLOCK
    shift = n_c - LEAD // GLA_BLOCK
    proj = proj.reshape(batch, length, proj.shape[-1])
    a = a.reshape(batch, length, LANES)

    def blk(col):
        return lambda b, c: (b, (c + shift) % n_c, col)

    (o,), cast = _host_call(
        functools.partial(_gla_core_kernel, dk=dk, dv=dv),
        make_casts(lambda b, c: b * n_c + c, batch * n_c),
        in_specs=[
            pl.BlockSpec((1, GLA_BLOCK, kd), blk(0)),
            pl.BlockSpec((1, GLA_BLOCK, kd), blk(1)),
            pl.BlockSpec((1, GLA_BLOCK, vd), blk(kd * 2 // vd)),
            pl.BlockSpec((1, GLA_BLOCK, vd), blk(kd * 2 // vd + 1)),
            pl.BlockSpec((1, GLA_BLOCK, LANES), blk(0)),
            pl.BlockSpec((LANES, kd), lambda b, c: (0, 0)),
            pl.BlockSpec((1, kd), lambda b, c: (0, 0)),
            pl.BlockSpec((1, dv), lambda b, c: (0, 0)),
        ],
        out_specs=[pl.BlockSpec((1, GLA_BLOCK, vd), blk(0))],
        out_shape=[jax.ShapeDtypeStruct((batch, length, vd), BF16)],
        args=(proj, proj, proj, proj, a, w_gu, b_gate, g_onorm),
        grid=(batch, n_c),
        scratch_shapes=[pltpu.VMEM((GLA_HEADS, dv, dk), F32)],
        compiler_params=_params(("arbitrary", "arbitrary")),
        name="gla_core",
    )
    return o, cast


def _proj_res_kernel(x_ref, w_ref, res_ref, o_ref, *, side):
    zero = side()
    res = res_ref[0] if zero is None else res_ref[0] + zero
    o_ref[0] = res + _dot(x_ref[0], w_ref[...])


def _proj_res_lead_kernel(x_ref, w_ref, lead_ref, h_in, o_ref):
    del h_in
    o_ref[0] = lead_ref[...] + _dot(x_ref[0], w_ref[...])


def _proj_res_lead(x, w, lead, h):
    batch, length, n = h.shape
    d = x.shape[-1]
    blk = length // LEAD - 1
    return pl.pallas_call(
        _proj_res_lead_kernel,
        grid=(batch,),
        in_specs=[
            pl.BlockSpec((1, LEAD, d), lambda b: (b, blk, 0)),
            pl.BlockSpec((d, n), lambda b: (0, 0)),
            pl.BlockSpec((LEAD, n), lambda b: (0, 0)),
            pl.BlockSpec(memory_space=pl.ANY),
        ],
        out_specs=pl.BlockSpec((1, LEAD, n), lambda b: (b, blk, 0)),
        out_shape=jax.ShapeDtypeStruct(h.shape, h.dtype),
        input_output_aliases={3: 0},
        compiler_params=_params(("arbitrary",)),
        name="proj_res_lead",
    )(x, w, lead, h)


def _proj_res(x, w, res, rows, length, tm, tn, make_casts=lambda lin, n_steps: ()):
    batch, _, d = x.shape
    n = w.shape[1]
    n_i, n_j = rows // tm, n // tn
    (o,), cast = _host_call(
        _proj_res_kernel,
        make_casts(lambda b, i, j: (b * n_i + i) * n_j + j, batch * n_i * n_j),
        in_specs=[
            pl.BlockSpec((1, tm, d), lambda b, i, j: (b, i, 0)),
            pl.BlockSpec((d, tn), lambda b, i, j: (0, j)),
            pl.BlockSpec((1, tm, tn), lambda b, i, j: (b, i, j)),
        ],
        out_specs=[pl.BlockSpec((1, tm, tn), lambda b, i, j: (b, i, j))],
        out_shape=[jax.ShapeDtypeStruct((batch, length, n), F32)],
        args=(x, w, res),
        grid=(batch, n_i, n_j),
        compiler_params=_params(("arbitrary", "arbitrary", "arbitrary")),
        name="proj_res",
    )
    return o, cast


def _ffn_kernel(h_ref, g_ref, wg_ref, wu_ref, wd_ref, o_ref, y_scr, *, side):
    @pl.when(pl.program_id(2) == 0)
    def _():
        h = h_ref[0]
        y_scr[...] = _rms(h, g_ref[...]).astype(BF16)
        o_ref[0] = h

    zero = side()
    y = y_scr[...]
    gate = _dot(y, wg_ref[...])
    if zero is not None:
        gate = gate + zero
    up = _dot(y, wu_ref[...])
    act = (gate * jax.nn.sigmoid(gate) * up).astype(BF16)
    o_ref[0] += _dot(act, wd_ref[...])


def _ffn(h, g, w_gate, w_up, w_down, rows, tm, tf, make_casts=lambda lin, n_steps: ()):
    batch, _, d = h.shape
    d_ff = w_gate.shape[-1]
    n_i, n_f = rows // tm, d_ff // tf
    (o,), cast = _host_call(
        _ffn_kernel,
        make_casts(lambda b, i, f: (b * n_i + i) * n_f + f, batch * n_i * n_f),
        in_specs=[
            pl.BlockSpec((1, tm, d), lambda b, i, f: (b, i, 0)),
            pl.BlockSpec((1, d), lambda b, i, f: (0, 0)),
            pl.BlockSpec((d, tf), lambda b, i, f: (0, f)),
            pl.BlockSpec((d, tf), lambda b, i, f: (0, f)),
            pl.BlockSpec((tf, d), lambda b, i, f: (f, 0)),
        ],
        out_specs=[pl.BlockSpec((1, tm, d), lambda b, i, f: (b, i, 0))],
        out_shape=[jax.ShapeDtypeStruct((batch, rows, d), F32)],
        args=(h, g, w_gate, w_up, w_down),
        grid=(batch, n_i, n_f),
        scratch_shapes=[pltpu.VMEM((tm, d), BF16)],
        compiler_params=_params(("arbitrary", "arbitrary", "arbitrary")),
        name="ffn",
    )
    return o, cast


def _kv_proj_kernel(h_ref, g_ref, w_ref, gk_ref, o_ref, y_scr, *, n_key_tiles):
    j = pl.program_id(2)

    @pl.when(j == 0)
    def _():
        y_scr[...] = _rms(h_ref[0], g_ref[...]).astype(BF16)

    r = _dot(y_scr[...], w_ref[...])
    normed = jnp.concatenate(_head_rms(r, gk_ref[...], 1.0), axis=1)
    o_ref[0] = jnp.where(j < n_key_tiles, normed, r).astype(o_ref.dtype)


def _kv_proj(h, g, w_kv, g_k, tm, tn):
    batch, rows, d = h.shape
    n = w_kv.shape[1]
    return pl.pallas_call(
        functools.partial(_kv_proj_kernel, n_key_tiles=(n // 2) // tn),
        grid=(batch, rows // tm, n // tn),
        in_specs=[
            pl.BlockSpec((1, tm, d), lambda b, i, j: (b, i, 0)),
            pl.BlockSpec((1, d), lambda b, i, j: (0, 0)),
            pl.BlockSpec((d, tn), lambda b, i, j: (0, j)),
            pl.BlockSpec((1, SB_HEAD_DIM), lambda b, i, j: (0, 0)),
        ],
        out_specs=pl.BlockSpec((1, tm, tn), lambda b, i, j: (b, i, j)),
        out_shape=jax.ShapeDtypeStruct((batch, rows, n), BF16),
        scratch_shapes=[pltpu.VMEM((tm, d), BF16)],
        compiler_params=_params(("parallel", "parallel", "arbitrary")),
        name="kv_proj",
    )(h, g, w_kv, g_k)


def _q_proj_kernel(h_ref, g_ref, w_ref, gq_ref, o_ref, y_scr):
    @pl.when(pl.program_id(2) == 0)
    def _():
        y_scr[...] = _rms(h_ref[0], g_ref[...]).astype(BF16)

    r = _dot(y_scr[...], w_ref[...])
    heads = _head_rms(r, gq_ref[...], SB_HEAD_DIM ** -0.5)
    o_ref[0] = jnp.concatenate(heads, axis=1).astype(o_ref.dtype)


def _q_proj(h, g, w_q, g_q, rows, tm, tn):
    batch, _, d = h.shape
    n = w_q.shape[1]
    return pl.pallas_call(
        _q_proj_kernel,
        grid=(batch, rows // tm, n // tn),
        in_specs=[
            pl.BlockSpec((1, tm, d), lambda b, i, j: (b, i, 0)),
            pl.BlockSpec((1, d), lambda b, i, j: (0, 0)),
            pl.BlockSpec((d, tn), lambda b, i, j: (0, j)),
            pl.BlockSpec((1, SB_HEAD_DIM), lambda b, i, j: (0, 0)),
        ],
        out_specs=pl.BlockSpec((1, tm, tn), lambda b, i, j: (b, i, j)),
        out_shape=jax.ShapeDtypeStruct((batch, rows, n), BF16),
        scratch_shapes=[pltpu.VMEM((tm, d), BF16)],
        compiler_params=_params(("parallel", "parallel", "arbitrary")),
        name="q_proj",
    )(h, g, w_q, g_q)


def _sb_scores(q_ref, k_ref, z_scr, rs, start):
    d = SB_HEAD_DIM
    for hd in range(SB_HG):
        hs = slice(hd * d, (hd + 1) * d)
        z_scr[hd] = _dot_nt(q_ref[0, rs, hs], k_ref[0, pl.ds(start, d), hs])


def _sb_sweep(q_ref, k_ref, v_ref, u, acc_scr, car_scr, z_scr, rs, start, next_start, mask):
    d = SB_HEAD_DIM
    cats, logits = [], []
    for hd in range(SB_HG):
        z = z_scr[hd]
        cost = jnp.maximum(z, 0.0) + jnp.log(1.0 + jnp.exp2(jnp.abs(z) * -LOG2E))
        if mask is not None:
            cost = jnp.where(mask, cost, 0.0)
        hi, lo = _split_bf16(cost)
        cats.append(jnp.concatenate([hi, lo], axis=1))
        logits.append(z - cost)
    sums = [_dot(c, u) for c in cats]
    if next_start is not None:
        _sb_scores(q_ref, k_ref, z_scr, rs, next_start)
    atts = []
    spent = None
    for hd in range(SB_HG):
        carry = car_scr[hd]
        att = jnp.exp(logits[hd] - sums[hd][:, :d] - carry)
        if mask is not None:
            att = jnp.where(mask, att, 0.0)
        atts.append(att.astype(BF16))
        carry = carry + sums[hd][:, d:]
        car_scr[hd] = carry
        spent = carry if spent is None else jnp.minimum(spent, carry)
    for hd in range(SB_HG):
        acc_scr[hd] += _dot(atts[hd], v_ref[0, pl.ds(start, d), hd * d:(hd + 1) * d])
    return jnp.min(spent)


def _sb_kernel(q_ref, k_ref, v_ref, u_ref, o_ref, acc_scr, car_scr, z_scr, *, lead_start):
    d = SB_HEAD_DIM
    row = lax.broadcasted_iota(jnp.int32, (d, d), 0)
    col = lax.broadcasted_iota(jnp.int32, (d, d), 1)
    u = u_ref[...]

    def block_start(jb):
        return pl.multiple_of(jnp.where(jb >= 0, jb * d, lead_start), d)

    def cond(state):
        jb, spent = state
        return jnp.logical_and(jb >= 0, spent < SB_DEAD_COST)

    for t in range(SB_TQ // d):
        rs = slice(t * d, (t + 1) * d)
        sweep = functools.partial(_sb_sweep, q_ref, k_ref, v_ref, u, acc_scr, car_scr, z_scr, rs)
        acc_scr[...] = jnp.zeros_like(acc_scr)
        car_scr[...] = jnp.zeros_like(car_scr)
        i = pl.program_id(2) * (SB_TQ // d) + t
        _sb_scores(q_ref, k_ref, z_scr, rs, block_start(i))
        spent0 = sweep(block_start(i), block_start(i - 1), col < row)

        def body(state, sweep=sweep):
            jb, _ = state
            return jb - 1, sweep(block_start(jb), block_start(jb - 1), None)

        jb, spent = lax.while_loop(cond, body, (i - 1, spent0))

        @pl.when(jnp.logical_and(jb < 0, spent < SB_DEAD_COST))
        def _(sweep=sweep):
            sweep(lead_start, None, col >= PAD)

        for hd in range(SB_HG):
            o_ref[0, rs, hd * d:(hd + 1) * d] = acc_scr[hd].astype(o_ref.dtype)


def _sb_attn(q, kv, u):
    batch, rows, dm = q.shape
    length = kv.shape[1]
    gw = SB_HG * SB_HEAD_DIM
    return pl.pallas_call(
        functools.partial(_sb_kernel, lead_start=rows),
        grid=(batch, dm // gw, rows // SB_TQ),
        in_specs=[
            pl.BlockSpec((1, SB_TQ, gw), lambda b, h, i: (b, i, h)),
            pl.BlockSpec((1, length, gw), lambda b, h, i: (b, 0, h)),
            pl.BlockSpec((1, length, gw), lambda b, h, i: (b, 0, h + dm // gw)),
            pl.BlockSpec((2 * SB_HEAD_DIM, 2 * SB_HEAD_DIM), lambda b, h, i: (0, 0)),
        ],
        out_specs=pl.BlockSpec((1, SB_TQ, gw), lambda b, h, i: (b, i, h)),
        out_shape=jax.ShapeDtypeStruct((batch, rows, dm), BF16),
        scratch_shapes=[pltpu.VMEM((SB_HG, SB_HEAD_DIM, SB_HEAD_DIM), F32)] * 3,
        compiler_params=_params(("parallel", "parallel", "arbitrary")),
        name="sb_attn",
    )(q, kv, kv, u)


def _suffix_matrix():
    r = jnp.arange(SB_HEAD_DIM)
    later = (r[:, None] > r[None, :]).astype(BF16)
    half = jnp.concatenate([later, jnp.ones((SB_HEAD_DIM, SB_HEAD_DIM), BF16)], axis=1)
    return jnp.concatenate([half, half], axis=0)


def kernel(x, meta_tokens, g_norm_a, w_in_a, w_gate_up_a, b_gate_a, g_onorm_a, w_out_a, g_kv_norm, w_kv, g_k, g_norm_b, w_q_b, g_q_b, w_o_b, g_ffn_norm, w_ffn_gate, w_ffn_up, w_ffn_down):
    batch, seq, d = x.shape
    length = seq + LEAD
    n_a = g_norm_a.shape[0]
    n_b = g_norm_b.shape[0]
    n_main = w_in_a.shape[-1] - GATE_RANK

    lead = jnp.concatenate([jnp.zeros((PAD, d), x.dtype), meta_tokens.astype(x.dtype)], axis=0)
    h = None
    if not n_a:
        h = jnp.concatenate([x, jnp.broadcast_to(lead[None], (batch, LEAD, d))], axis=1)

    def row(v):
        return v.reshape(1, -1).astype(F32)

    ready = {}

    def bf(name, w, layer=None):
        if (name, layer) in ready:
            return ready.pop((name, layer))
        return (w if layer is None else w[layer]).astype(BF16)

    def plan(jobs):
        def make_casts(lin, n_steps):
            return [_cast_job(w, layer, block, lin, n_steps) for _, w, layer, block in jobs]

        def collect(converted):
            for (name, _, layer, _), c in zip(jobs, converted):
                ready[(name, layer)] = c
        return make_casts, collect

    def ffn_jobs(layer, in_block, down_block):
        return [("gate", w_ffn_gate, layer, in_block), ("up", w_ffn_up, layer, in_block),
                ("down", w_ffn_down, layer, down_block)]

    d_ff = w_ffn_gate.shape[-1]

    def ffn_weights(layer):
        return (bf("gate", w_ffn_gate, layer), bf("up", w_ffn_up, layer),
                bf("down", w_ffn_down, layer))

    w_in_bf = w_in_a.astype(BF16)
    for layer in range(n_a):
        last_a = layer == n_a - 1
        w_a = jnp.pad(w_in_bf[layer][:, n_main:], ((0, 0), (0, LANES - GATE_RANK)))
        w_gu = jnp.pad(w_gate_up_a[layer], ((0, LANES - GATE_RANK), (0, 0))).astype(BF16)
        g_a = row(g_norm_a[layer])
        make_casts, collect = plan(ffn_jobs(layer, (d, LANES), (LANES, d)))
        if h is None:
            (proj, a), converted = _gla_proj(x, g_a, w_in_bf, layer, w_a, rows=seq, length=length,
                                             tm=1024, tn=1024, make_casts=make_casts)
            proj, a = _gla_proj_lead(lead, g_a, w_in_bf, layer, w_a, proj, a, tn=1024)
        else:
            (proj, a), converted = _gla_proj(h, g_a, w_in_bf, layer, w_a, rows=length,
                                             length=length, tm=1056, tn=1024,
                                             make_casts=make_casts)
        collect(converted)
        make_casts, collect = plan([("out", w_out_a, layer, (128, 1024))])
        o, converted = _gla_core(proj, a, w_gu, row(b_gate_a[layer]), row(g_onorm_a[layer]),
                                 batch, length, make_casts)
        collect(converted)
        w_out = bf("out", w_out_a, layer)
        make_casts, collect = plan([("kv", w_kv, None, (128, w_kv.shape[1]))] if last_a else [])
        if h is None:
            h, converted = _proj_res(o, w_out, x, rows=seq, length=length, tm=512, tn=d,
                                     make_casts=make_casts)
            h = _proj_res_lead(o, w_out, lead, h)
        else:
            h, converted = _proj_res(o, w_out, h, rows=length, length=length, tm=528, tn=d,
                                     make_casts=make_casts)
        collect(converted)
        jobs = []
        if last_a and n_b:
            jobs = ([("q", w_q_b, 0, (128, 1024)), ("o", w_o_b, 0, (128, 1024))]
                    + ffn_jobs(n_a, (32, d_ff), (64, d)))
        make_casts, collect = plan(jobs)
        h, converted = _ffn(h, row(g_ffn_norm[layer]), *ffn_weights(layer),
                            rows=length, tm=704, tf=512, make_casts=make_casts)
        collect(converted)

    kv = _kv_proj(h, row(g_kv_norm), bf("kv", w_kv), row(g_k), tm=1056, tn=1024)
    u = _suffix_matrix()
    for j in range(n_b):
        layer = n_a + j
        q = _q_proj(h, row(g_norm_b[j]), bf("q", w_q_b, j), row(g_q_b[j]),
                    rows=seq, tm=1024, tn=1024)
        o = _sb_attn(q, kv, u)
        h, _ = _proj_res(o, bf("o", w_o_b, j), h, rows=seq, length=seq, tm=512, tn=d)
        jobs = []
        if j + 1 < n_b:
            jobs = ([("q", w_q_b, j + 1, (128, 1024)), ("o", w_o_b, j + 1, (128, 1024))]
                    + ffn_jobs(layer + 1, (32, d_ff), (64, d)))
        make_casts, collect = plan(jobs)
        h, converted = _ffn(h, row(g_ffn_norm[layer]), *ffn_weights(layer),
                            rows=seq, tm=512, tf=512, make_casts=make_casts)
        collect(converted)
    return h
```

```python
import functools
from typing import NamedTuple

import jax
import jax.numpy as jnp
from jax import lax
from jax.experimental import pallas as pl
from jax.experimental.pallas import tpu as pltpu

F32 = jnp.float32
BF16 = jnp.bfloat16

CHUNK = 64
N_META = 16
LEAD = 128
PAD = LEAD - N_META
GLA_HEADS = 4
GLA_BLOCK = 2 * CHUNK
GATE_RANK = 16
GATE_TAU = 16.0
SB_HEAD_DIM = 128
EPS = 1e-6

LANES = 128
VMEM_LIMIT = 56 * 1024 * 1024

SB_DEAD_COST = 88.0
LOG2E = 1.4426950408889634
SB_TQ = 4 * SB_HEAD_DIM
SB_HG = 8


def _params(sem):
    return pltpu.CompilerParams(dimension_semantics=sem, vmem_limit_bytes=VMEM_LIMIT)


def _dot(a, b):
    return jnp.dot(a, b, preferred_element_type=F32)


def _dot_nt(a, b):
    return lax.dot_general(a, b, (((1,), (1,)), ((), ())), preferred_element_type=F32)


def _dot_tn(a, b):
    return lax.dot_general(a, b, (((0,), (0,)), ((), ())), preferred_element_type=F32)


def _rms(x, g):
    ms = jnp.mean(x * x, axis=-1, keepdims=True)
    return x * lax.rsqrt(ms + EPS) * g


def _log_sigmoid(x):
    return jnp.minimum(x, 0.0) - jnp.log(1.0 + jnp.exp(-jnp.abs(x)))


def _split_bf16(x):
    hi = x.astype(BF16)
    lo = (x - hi.astype(F32)).astype(BF16)
    return hi, lo


_PROBE_ROWS = 16


class _Cast(NamedTuple):
    src: jax.Array
    in_spec: pl.BlockSpec
    out_spec: pl.BlockSpec
    out_shape: jax.ShapeDtypeStruct


def _cast_job(w, layer, block, lin, n_steps):
    rows, cols = w.shape[-2:]
    br, bc = block
    n_c = cols // bc
    n_blocks = (rows // br) * n_c
    assert rows % br == 0 and cols % bc == 0 and n_blocks <= n_steps
    assert br % _PROBE_ROWS == 0 and br >= 2 * _PROBE_ROWS

    def idx(*g):
        blk = lin(*g) * n_blocks // n_steps
        return blk // n_c, blk % n_c

    if w.ndim == 3:
        in_spec = pl.BlockSpec((None, br, bc), lambda *g: (layer, *idx(*g)))
    else:
        in_spec = pl.BlockSpec((br, bc), idx)
    return _Cast(w, in_spec, pl.BlockSpec((br, bc), idx), jax.ShapeDtypeStruct((rows, cols), BF16))


def _hosting(body, n_in, n_out, n_cast):
    def kernel(*refs):
        a, b, c = n_in, n_in + n_cast, n_in + n_cast + n_out

        def side():
            if not n_cast:
                return None
            off = pl.multiple_of((pl.program_id(0) % 2) * _PROBE_ROWS, _PROBE_ROWS)
            bits = None
            for src, dst in zip(refs[a:b], refs[c:c + n_cast]):
                dst[...] = src[...].astype(dst.dtype)
                probe = pltpu.bitcast(dst[pl.ds(off, _PROBE_ROWS), :], jnp.uint32)
                for l in range(0, probe.shape[1], LANES):
                    piece = probe[:, l:l + LANES]
                    bits = piece if bits is None else bits | piece
            zero = pltpu.bitcast((bits >> 16) >> 16, F32)
            return jnp.max(zero, axis=(0, 1), keepdims=True)

        body(*refs[:a], *refs[b:c], *refs[c + n_cast:], side=side)
    return kernel


def _host_call(body, casts, in_specs, out_specs, out_shape, args, **kwargs):
    n_out = len(out_specs)
    outs = pl.pallas_call(
        _hosting(body, len(in_specs), n_out, len(casts)),
        in_specs=list(in_specs) + [c.in_spec for c in casts],
        out_specs=list(out_specs) + [c.out_spec for c in casts],
        out_shape=list(out_shape) + [c.out_shape for c in casts],
        **kwargs,
    )(*args, *[c.src for c in casts])
    return outs[:n_out], outs[n_out:]


def _head_rms(x, g, scale):
    outs = []
    for h in range(x.shape[1] // LANES):
        xs = x[:, h * LANES:(h + 1) * LANES]
        ms = jnp.mean(xs * xs, axis=-1, keepdims=True)
        outs.append(xs * lax.rsqrt(ms + EPS) * g * scale)
    return outs


def _gla_proj_kernel(h_ref, g_ref, w_ref, wa_ref, proj_ref, a_ref, y_scr, *, side):
    @pl.when(pl.program_id(2) == 0)
    def _():
        y = _rms(h_ref[0], g_ref[...]).astype(BF16)
        y_scr[...] = y
        a_ref[0] = _dot(y, wa_ref[...])

    zero = side()
    r = _dot(y_scr[...], w_ref[...])
    proj_ref[0] = r if zero is None else r + zero


def _gla_proj(h, g, w_in, layer, w_a, rows, length, tm, tn, make_casts):
    batch, _, d = h.shape
    n = w_in.shape[-1] - GATE_RANK
    n_i, n_j = rows // tm, n // tn
    return _host_call(
        _gla_proj_kernel,
        make_casts(lambda b, i, j: (b * n_i + i) * n_j + j, batch * n_i * n_j),
        in_specs=[
            pl.BlockSpec((1, tm, d), lambda b, i, j: (b, i, 0)),
            pl.BlockSpec((1, d), lambda b, i, j: (0, 0)),
            pl.BlockSpec((None, d, tn), lambda b, i, j: (layer, 0, j)),
            pl.BlockSpec((d, LANES), lambda b, i, j: (0, 0)),
        ],
        out_specs=[
            pl.BlockSpec((1, tm, tn), lambda b, i, j: (b, i, j)),
            pl.BlockSpec((1, tm, LANES), lambda b, i, j: (b, i, 0)),
        ],
        out_shape=[
            jax.ShapeDtypeStruct((batch, length, n), F32),
            jax.ShapeDtypeStruct((batch, length, LANES), F32),
        ],
        args=(h, g, w_in, w_a),
        grid=(batch, n_i, n_j),
        scratch_shapes=[pltpu.VMEM((tm, d), BF16)],
        compiler_params=_params(("arbitrary", "arbitrary", "arbitrary")),
        name="gla_proj",
    )


def _gla_proj_lead_kernel(lead_ref, g_ref, w_ref, wa_ref, proj_in, a_in, proj_ref, a_ref):
    del proj_in, a_in
    y = _rms(lead_ref[...], g_ref[...]).astype(BF16)
    a = _dot(y, wa_ref[...])
    r = _dot(y, w_ref[...])
    for b in range(proj_ref.shape[0]):
        a_ref[b] = a
        proj_ref[b] = r


def _gla_proj_lead(lead, g, w_in, layer, w_a, proj, a, tn):
    batch, length, n = proj.shape
    d = lead.shape[1]
    blk = length // LEAD - 1
    return pl.pallas_call(
        _gla_proj_lead_kernel,
        grid=(n // tn,),
        in_specs=[
            pl.BlockSpec((LEAD, d), lambda j: (0, 0)),
            pl.BlockSpec((1, d), lambda j: (0, 0)),
            pl.BlockSpec((None, d, tn), lambda j: (layer, 0, j)),
            pl.BlockSpec((d, LANES), lambda j: (0, 0)),
            pl.BlockSpec(memory_space=pl.ANY),
            pl.BlockSpec(memory_space=pl.ANY),
        ],
        out_specs=[
            pl.BlockSpec((batch, LEAD, tn), lambda j: (0, blk, j)),
            pl.BlockSpec((batch, LEAD, LANES), lambda j: (0, blk, 0)),
        ],
        out_shape=[jax.ShapeDtypeStruct(proj.shape, proj.dtype),
                   jax.ShapeDtypeStruct(a.shape, a.dtype)],
        input_output_aliases={4: 0, 5: 1},
        compiler_params=_params(("arbitrary",)),
        name="gla_proj_lead",
    )(lead, g, w_in, w_a, proj, a)


def _gla_core_kernel(q_ref, k_ref, v_ref, g_ref, a_ref, a_next_ref, wgu_ref, bg_ref, gon_ref, o_ref,
                     st_scr, la_scr, *, dk, dv, side):
    blk = pl.program_id(1)
    rows = q_ref.shape[1]

    def log_gate(a):
        return _log_sigmoid(_dot(a.astype(BF16), wgu_ref[...]) + bg_ref[...]) * (1.0 / GATE_TAU)

    @pl.when(blk == 0)
    def _():
        st_scr[...] = jnp.zeros_like(st_scr)
        la_scr[...] = log_gate(a_ref[0])

    zero = side()

    row = lax.broadcasted_iota(jnp.int32, (rows, rows), 0)
    col = lax.broadcasted_iota(jnp.int32, (rows, rows), 1)
    same_chunk = (row ^ col) < CHUNK
    tri = jnp.logical_and(col <= row, same_chunk).astype(BF16)
    lower = (lax.broadcasted_iota(jnp.int32, (CHUNK, CHUNK), 0)
             >= lax.broadcasted_iota(jnp.int32, (CHUNK, CHUNK), 1))
    chunk_pos = blk * rows + lax.broadcasted_iota(jnp.int32, (CHUNK, 1), 0)

    la_hi, la_lo = _split_bf16(la_scr[...])
    bcum_all = _dot(tri, la_hi) + _dot(tri, la_lo)
    gon = gon_ref[...]

    for sub in range(rows // CHUNK):
        rs = slice(sub * CHUNK, (sub + 1) * CHUNK)
        key_ok = chunk_pos + sub * CHUNK >= PAD
        for hd in range(GLA_HEADS):
            ks = slice(hd * dk, (hd + 1) * dk)
            vs = slice(hd * dv, (hd + 1) * dv)
            q = q_ref[0, rs, ks] * (dk ** -0.5)
            k = jnp.where(key_ok, k_ref[0, rs, ks], 0.0)
            v = v_ref[0, rs, vs].astype(BF16)
            b = bcum_all[rs, ks]
            b_ref = b[CHUNK // 2 - 1:CHUNK // 2, :]
            b_last = b[CHUNK - 1:CHUNK, :]
            e_pos = jnp.exp(b - b_ref)
            e_neg = jnp.exp(b_ref - b)
            s_lo = _dot_nt((q * e_pos).astype(BF16), (k * e_neg).astype(BF16))
            s_up = _dot_nt((q * e_neg).astype(BF16), (k * e_pos).astype(BF16))
            st = st_scr[hd]
            o_inter = _dot_nt((q * jnp.exp(b)).astype(BF16), st.astype(BF16))
            k_state = (k * jnp.exp(b_last - b)).astype(BF16)
            st_scr[hd] = st * jnp.exp(b_last) + _dot_tn(v, k_state)
            scores = jnp.where(lower, s_lo, s_up).astype(BF16)
            o = _dot(scores, v) + o_inter
            if zero is not None and sub == 0 and hd == GLA_HEADS // 2:
                o = o + zero
            o = _rms(o, gon)
            gate = g_ref[0, rs, vs]
            o_ref[0, rs, vs] = (o * (gate * jax.nn.sigmoid(gate))).astype(o_ref.dtype)

    la_scr[...] = log_gate(a_next_ref[0])


def _gla_core(proj, a, w_gu, b_gate, g_onorm, batch, length, make_casts):
    kd = w_gu.shape[-1]
    vd = (proj.shape[-1] - 2 * kd) // 2
    dk, dv = kd // GLA_HEADS, vd // GLA_HEADS
    n_c = length // GLA_BLOCK
    shift = n_c - LEAD // GLA_BLOCK
    proj = proj.reshape(batch, length, proj.shape[-1])
    a = a.reshape(batch, length, LANES)

    def blk(col):
        return lambda b, c: (b, (c + shift) % n_c, col)

    (o,), cast = _host_call(
        functools.partial(_gla_core_kernel, dk=dk, dv=dv),
        make_casts(lambda b, c: b * n_c + c, batch * n_c),
        in_specs=[
            pl.BlockSpec((1, GLA_BLOCK, kd), blk(0)),
            pl.BlockSpec((1, GLA_BLOCK, kd), blk(1)),
            pl.BlockSpec((1, GLA_BLOCK, vd), blk(kd * 2 // vd)),
            pl.BlockSpec((1, GLA_BLOCK, vd), blk(kd * 2 // vd + 1)),
            pl.BlockSpec((1, GLA_BLOCK, LANES), blk(0)),
            pl.BlockSpec((1, GLA_BLOCK, LANES),
                         lambda b, c: (b, ((c + 1) % n_c + shift) % n_c, 0)),
            pl.BlockSpec((LANES, kd), lambda b, c: (0, 0)),
            pl.BlockSpec((1, kd), lambda b, c: (0, 0)),
            pl.BlockSpec((1, dv), lambda b, c: (0, 0)),
        ],
        out_specs=[pl.BlockSpec((1, GLA_BLOCK, vd), blk(0))],
        out_shape=[jax.ShapeDtypeStruct((batch, length, vd), BF16)],
        args=(proj, proj, proj, proj, a, a, w_gu, b_gate, g_onorm),
        grid=(batch, n_c),
        scratch_shapes=[pltpu.VMEM((GLA_HEADS, dv, dk), F32), pltpu.VMEM((GLA_BLOCK, kd), F32)],
        compiler_params=_params(("arbitrary", "arbitrary")),
        name="gla_core",
    )
    return o, cast


def _proj_res_kernel(x_ref, w_ref, res_ref, o_ref, *, side):
    zero = side()
    res = res_ref[0] if zero is None else res_ref[0] + zero
    o_ref[0] = res + _dot(x_ref[0], w_ref[...])


def _proj_res_lead_kernel(x_ref, w_ref, lead_ref, h_in, o_ref):
    del h_in
    o_ref[0] = lead_ref[...] + _dot(x_ref[0], w_ref[...])


def _proj_res_lead(x, w, lead, h):
    batch, length, n = h.shape
    d = x.shape[-1]
    blk = length // LEAD - 1
    return pl.pallas_call(
        _proj_res_lead_kernel,
        grid=(batch,),
        in_specs=[
            pl.BlockSpec((1, LEAD, d), lambda b: (b, blk, 0)),
            pl.BlockSpec((d, n), lambda b: (0, 0)),
            pl.BlockSpec((LEAD, n), lambda b: (0, 0)),
            pl.BlockSpec(memory_space=pl.ANY),
        ],
        out_specs=pl.BlockSpec((1, LEAD, n), lambda b: (b, blk, 0)),
        out_shape=jax.ShapeDtypeStruct(h.shape, h.dtype),
        input_output_aliases={3: 0},
        compiler_params=_params(("arbitrary",)),
        name="proj_res_lead",
    )(x, w, lead, h)


def _proj_res(x, w, res, rows, length, tm, tn, make_casts=lambda lin, n_steps: ()):
    batch, _, d = x.shape
    n = w.shape[1]
    n_i, n_j = rows // tm, n // tn
    (o,), cast = _host_call(
        _proj_res_kernel,
        make_casts(lambda b, i, j: (b * n_i + i) * n_j + j, batch * n_i * n_j),
        in_specs=[
            pl.BlockSpec((1, tm, d), lambda b, i, j: (b, i, 0)),
            pl.BlockSpec((d, tn), lambda b, i, j: (0, j)),
            pl.BlockSpec((1, tm, tn), lambda b, i, j: (b, i, j)),
        ],
        out_specs=[pl.BlockSpec((1, tm, tn), lambda b, i, j: (b, i, j))],
        out_shape=[jax.ShapeDtypeStruct((batch, length, n), F32)],
        args=(x, w, res),
        grid=(batch, n_i, n_j),
        compiler_params=_params(("arbitrary", "arbitrary", "arbitrary")),
        name="proj_res",
    )
    return o, cast


def _ffn_kernel(h_ref, g_ref, wg_ref, wu_ref, wd_ref, o_ref, y_scr, *, side):
    @pl.when(pl.program_id(2) == 0)
    def _():
        h = h_ref[0]
        y_scr[...] = _rms(h, g_ref[...]).astype(BF16)
        o_ref[0] = h

    zero = side()
    y = y_scr[...]
    gate = _dot(y, wg_ref[...])
    if zero is not None:
        gate = gate + zero
    up = _dot(y, wu_ref[...])
    act = (gate * jax.nn.sigmoid(gate) * up).astype(BF16)
    o_ref[0] += _dot(act, wd_ref[...])


def _ffn(h, g, w_gate, w_up, w_down, rows, tm, tf, make_casts=lambda lin, n_steps: ()):
    batch, _, d = h.shape
    d_ff = w_gate.shape[-1]
    n_i, n_f = rows // tm, d_ff // tf
    (o,), cast = _host_call(
        _ffn_kernel,
        make_casts(lambda b, i, f: (b * n_i + i) * n_f + f, batch * n_i * n_f),
        in_specs=[
            pl.BlockSpec((1, tm, d), lambda b, i, f: (b, i, 0)),
            pl.BlockSpec((1, d), lambda b, i, f: (0, 0)),
            pl.BlockSpec((d, tf), lambda b, i, f: (0, f)),
            pl.BlockSpec((d, tf), lambda b, i, f: (0, f)),
            pl.BlockSpec((tf, d), lambda b, i, f: (f, 0)),
        ],
        out_specs=[pl.BlockSpec((1, tm, d), lambda b, i, f: (b, i, 0))],
        out_shape=[jax.ShapeDtypeStruct((batch, rows, d), F32)],
        args=(h, g, w_gate, w_up, w_down),
        grid=(batch, n_i, n_f),
        scratch_shapes=[pltpu.VMEM((tm, d), BF16)],
        compiler_params=_params(("arbitrary", "arbitrary", "arbitrary")),
        name="ffn",
    )
    return o, cast


def _kv_proj_kernel(h_ref, g_ref, w_ref, gk_ref, o_ref, y_scr, *, n_key_tiles):
    j = pl.program_id(2)

    @pl.when(j == 0)
    def _():
        y_scr[...] = _rms(h_ref[0], g_ref[...]).astype(BF16)

    r = _dot(y_scr[...], w_ref[...])
    normed = jnp.concatenate(_head_rms(r, gk_ref[...], 1.0), axis=1)
    o_ref[0] = jnp.where(j < n_key_tiles, normed, r).astype(o_ref.dtype)


def _kv_proj(h, g, w_kv, g_k, tm, tn):
    batch, rows, d = h.shape
    n = w_kv.shape[1]
    return pl.pallas_call(
        functools.partial(_kv_proj_kernel, n_key_tiles=(n // 2) // tn),
        grid=(batch, rows // tm, n // tn),
        in_specs=[
            pl.BlockSpec((1, tm, d), lambda b, i, j: (b, i, 0)),
            pl.BlockSpec((1, d), lambda b, i, j: (0, 0)),
            pl.BlockSpec((d, tn), lambda b, i, j: (0, j)),
            pl.BlockSpec((1, SB_HEAD_DIM), lambda b, i, j: (0, 0)),
        ],
        out_specs=pl.BlockSpec((1, tm, tn), lambda b, i, j: (b, i, j)),
        out_shape=jax.ShapeDtypeStruct((batch, rows, n), BF16),
        scratch_shapes=[pltpu.VMEM((tm, d), BF16)],
        compiler_params=_params(("parallel", "parallel", "arbitrary")),
        name="kv_proj",
    )(h, g, w_kv, g_k)


def _q_proj_kernel(h_ref, g_ref, w_ref, gq_ref, o_ref, y_scr):
    @pl.when(pl.program_id(2) == 0)
    def _():
        y_scr[...] = _rms(h_ref[0], g_ref[...]).astype(BF16)

    r = _dot(y_scr[...], w_ref[...])
    heads = _head_rms(r, gq_ref[...], SB_HEAD_DIM ** -0.5)
    o_ref[0] = jnp.concatenate(heads, axis=1).astype(o_ref.dtype)


def _q_proj(h, g, w_q, g_q, rows, tm, tn):
    batch, _, d = h.shape
    n = w_q.shape[1]
    return pl.pallas_call(
        _q_proj_kernel,
        grid=(batch, rows // tm, n // tn),
        in_specs=[
            pl.BlockSpec((1, tm, d), lambda b, i, j: (b, i, 0)),
            pl.BlockSpec((1, d), lambda b, i, j: (0, 0)),
            pl.BlockSpec((d, tn), lambda b, i, j: (0, j)),
            pl.BlockSpec((1, SB_HEAD_DIM), lambda b, i, j: (0, 0)),
        ],
        out_specs=pl.BlockSpec((1, tm, tn), lambda b, i, j: (b, i, j)),
        out_shape=jax.ShapeDtypeStruct((batch, rows, n), BF16),
        scratch_shapes=[pltpu.VMEM((tm, d), BF16)],
        compiler_params=_params(("parallel", "parallel", "arbitrary")),
        name="q_proj",
    )(h, g, w_q, g_q)


def _sb_scores(q_ref, k_ref, z_scr, rs, start):
    d = SB_HEAD_DIM
    for hd in range(SB_HG):
        hs = slice(hd * d, (hd + 1) * d)
        z_scr[hd] = _dot_nt(q_ref[0, rs, hs], k_ref[0, pl.ds(start, d), hs])


def _sb_sweep(q_ref, k_ref, v_ref, u, acc_scr, car_scr, z_scr, rs, start, next_start, mask):
    d = SB_HEAD_DIM
    cats, logits = [], []
    for hd in range(SB_HG):
        z = z_scr[hd]
        cost = jnp.maximum(z, 0.0) + jnp.log(1.0 + jnp.exp2(jnp.abs(z) * -LOG2E))
        if mask is not None:
            cost = jnp.where(mask, cost, 0.0)
        hi, lo = _split_bf16(cost)
        cats.append(jnp.concatenate([hi, lo], axis=1))
        logits.append(z - cost)
    sums = [_dot(c, u) for c in cats]
    if next_start is not None:
        _sb_scores(q_ref, k_ref, z_scr, rs, next_start)
    atts = []
    spent = None
    for hd in range(SB_HG):
        carry = car_scr[hd]
        att = jnp.exp(logits[hd] - sums[hd][:, :d] - carry)
        if mask is not None:
            att = jnp.where(mask, att, 0.0)
        atts.append(att.astype(BF16))
        carry = carry + sums[hd][:, d:]
        car_scr[hd] = carry
        spent = carry if spent is None else jnp.minimum(spent, carry)
    for hd in range(SB_HG):
        acc_scr[hd] += _dot(atts[hd], v_ref[0, pl.ds(start, d), hd * d:(hd + 1) * d])
    return jnp.min(spent)


def _sb_kernel(q_ref, k_ref, v_ref, u_ref, o_ref, acc_scr, car_scr, z_scr, *, lead_start):
    d = SB_HEAD_DIM
    row = lax.broadcasted_iota(jnp.int32, (d, d), 0)
    col = lax.broadcasted_iota(jnp.int32, (d, d), 1)
    u = u_ref[...]

    def block_start(jb):
        return pl.multiple_of(jnp.where(jb >= 0, jb * d, lead_start), d)

    def cond(state):
        jb, spent = state
        return jnp.logical_and(jb >= 0, spent < SB_DEAD_COST)

    for t in range(SB_TQ // d):
        rs = slice(t * d, (t + 1) * d)
        sweep = functools.partial(_sb_sweep, q_ref, k_ref, v_ref, u, acc_scr, car_scr, z_scr, rs)
        acc_scr[...] = jnp.zeros_like(acc_scr)
        car_scr[...] = jnp.zeros_like(car_scr)
        i = pl.program_id(2) * (SB_TQ // d) + t
        _sb_scores(q_ref, k_ref, z_scr, rs, block_start(i))
        spent0 = sweep(block_start(i), block_start(i - 1), col < row)

        def body(state, sweep=sweep):
            jb, _ = state
            return jb - 1, sweep(block_start(jb), block_start(jb - 1), None)

        jb, spent = lax.while_loop(cond, body, (i - 1, spent0))

        @pl.when(jnp.logical_and(jb < 0, spent < SB_DEAD_COST))
        def _(sweep=sweep):
            sweep(lead_start, None, col >= PAD)

        for hd in range(SB_HG):
            o_ref[0, rs, hd * d:(hd + 1) * d] = acc_scr[hd].astype(o_ref.dtype)


def _sb_attn(q, kv, u):
    batch, rows, dm = q.shape
    length = kv.shape[1]
    gw = SB_HG * SB_HEAD_DIM
    return pl.pallas_call(
        functools.partial(_sb_kernel, lead_start=rows),
        grid=(batch, dm // gw, rows // SB_TQ),
        in_specs=[
            pl.BlockSpec((1, SB_TQ, gw), lambda b, h, i: (b, i, h)),
            pl.BlockSpec((1, length, gw), lambda b, h, i: (b, 0, h)),
            pl.BlockSpec((1, length, gw), lambda b, h, i: (b, 0, h + dm // gw)),
            pl.BlockSpec((2 * SB_HEAD_DIM, 2 * SB_HEAD_DIM), lambda b, h, i: (0, 0)),
        ],
        out_specs=pl.BlockSpec((1, SB_TQ, gw), lambda b, h, i: (b, i, h)),
        out_shape=jax.ShapeDtypeStruct((batch, rows, dm), BF16),
        scratch_shapes=[pltpu.VMEM((SB_HG, SB_HEAD_DIM, SB_HEAD_DIM), F32)] * 3,
        compiler_params=_params(("parallel", "parallel", "arbitrary")),
        name="sb_attn",
    )(q, kv, kv, u)


def _suffix_matrix():
    r = jnp.arange(SB_HEAD_DIM)
    later = (r[:, None] > r[None, :]).astype(BF16)
    half = jnp.concatenate([later, jnp.ones((SB_HEAD_DIM, SB_HEAD_DIM), BF16)], axis=1)
    return jnp.concatenate([half, half], axis=0)


def kernel(x, meta_tokens, g_norm_a, w_in_a, w_gate_up_a, b_gate_a, g_onorm_a, w_out_a, g_kv_norm, w_kv, g_k, g_norm_b, w_q_b, g_q_b, w_o_b, g_ffn_norm, w_ffn_gate, w_ffn_up, w_ffn_down):
    batch, seq, d = x.shape
    length = seq + LEAD
    n_a = g_norm_a.shape[0]
    n_b = g_norm_b.shape[0]
    n_main = w_in_a.shape[-1] - GATE_RANK

    lead = jnp.concatenate([jnp.zeros((PAD, d), x.dtype), meta_tokens.astype(x.dtype)], axis=0)
    h = None
    if not n_a:
        h = jnp.concatenate([x, jnp.broadcast_to(lead[None], (batch, LEAD, d))], axis=1)

    def row(v):
        return v.reshape(1, -1).astype(F32)

    ready = {}

    def bf(name, w, layer=None):
        if (name, layer) in ready:
            return ready.pop((name, layer))
        return (w if layer is None else w[layer]).astype(BF16)

    def plan(jobs):
        def make_casts(lin, n_steps):
            return [_cast_job(w, layer, block, lin, n_steps) for _, w, layer, block in jobs]

        def collect(converted):
            for (name, _, layer, _), c in zip(jobs, converted):
                ready[(name, layer)] = c
        return make_casts, collect

    def ffn_jobs(layer, in_block, down_block):
        return [("gate", w_ffn_gate, layer, in_block), ("up", w_ffn_up, layer, in_block),
                ("down", w_ffn_down, layer, down_block)]

    d_ff = w_ffn_gate.shape[-1]

    def ffn_weights(layer):
        return (bf("gate", w_ffn_gate, layer), bf("up", w_ffn_up, layer),
                bf("down", w_ffn_down, layer))

    w_in_bf = w_in_a.astype(BF16)
    for layer in range(n_a):
        last_a = layer == n_a - 1
        w_a = jnp.pad(w_in_bf[layer][:, n_main:], ((0, 0), (0, LANES - GATE_RANK)))
        w_gu = jnp.pad(w_gate_up_a[layer], ((0, LANES - GATE_RANK), (0, 0))).astype(BF16)
        g_a = row(g_norm_a[layer])
        make_casts, collect = plan(ffn_jobs(layer, (d, LANES), (LANES, d)))
        if h is None:
            (proj, a), converted = _gla_proj(x, g_a, w_in_bf, layer, w_a, rows=seq, length=length,
                                             tm=1024, tn=1024, make_casts=make_casts)
            proj, a = _gla_proj_lead(lead, g_a, w_in_bf, layer, w_a, proj, a, tn=1024)
        else:
            (proj, a), converted = _gla_proj(h, g_a, w_in_bf, layer, w_a, rows=length,
                                             length=length, tm=1056, tn=1024,
                                             make_casts=make_casts)
        collect(converted)
        make_casts, collect = plan([("out", w_out_a, layer, (128, 1024))])
        o, converted = _gla_core(proj, a, w_gu, row(b_gate_a[layer]), row(g_onorm_a[layer]),
                                 batch, length, make_casts)
        collect(converted)
        w_out = bf("out", w_out_a, layer)
        make_casts, collect = plan([("kv", w_kv, None, (128, w_kv.shape[1]))] if last_a else [])
        if h is None:
            h, converted = _proj_res(o, w_out, x, rows=seq, length=length, tm=512, tn=d,
                                     make_casts=make_casts)
            h = _proj_res_lead(o, w_out, lead, h)
        else:
            h, converted = _proj_res(o, w_out, h, rows=length, length=length, tm=528, tn=d,
                                     make_casts=make_casts)
        collect(converted)
        jobs = []
        if last_a and n_b:
            jobs = ([("q", w_q_b, 0, (128, 1024)), ("o", w_o_b, 0, (128, 1024))]
                    + ffn_jobs(n_a, (32, d_ff), (64, d)))
        make_casts, collect = plan(jobs)
        h, converted = _ffn(h, row(g_ffn_norm[layer]), *ffn_weights(layer),
                            rows=length, tm=704, tf=512, make_casts=make_casts)
        collect(converted)

    kv = _kv_proj(h, row(g_kv_norm), bf("kv", w_kv), row(g_k), tm=1056, tn=d)
    u = _suffix_matrix()
    for j in range(n_b):
        layer = n_a + j
        q = _q_proj(h, row(g_norm_b[j]), bf("q", w_q_b, j), row(g_q_b[j]),
                    rows=seq, tm=1024, tn=d)
        o = _sb_attn(q, kv, u)
        h, _ = _proj_res(o, bf("o", w_o_b, j), h, rows=seq, length=seq, tm=512, tn=d)
        jobs = []
        if j + 1 < n_b:
            jobs = ([("q", w_q_b, j + 1, (128, 1024)), ("o", w_o_b, j + 1, (128, 1024))]
                    + ffn_jobs(layer + 1, (32, d_ff), (64, d)))
        make_casts, collect = plan(jobs)
        h, converted = _ffn(h, row(g_ffn_norm[layer]), *ffn_weights(layer),
                            rows=seq, tm=512, tf=512, make_casts=make_casts)
        collect(converted)
    return h
```
